```python
import jax, jax.numpy as jnp
from jax import lax
import numpy as np

D_MODEL = 2048
BATCH = 4
SEQ = 2048
DEPTH = 4
DEC_BATCH = 128
DEC_SEQ = 1
PAST_LEN = 16384
PAGE_SIZE = 128

D_MIX = 2 * D_MODEL
RET_HEADS = 8
RET_DIM = D_MIX // 4
RET_HEAD_DIM = RET_DIM // RET_HEADS
SC_DIM = D_MIX // 4
SC_WIDTH = 3
SSM_DIM = D_MIX // 2
SSM_HEAD_DIM = 64
SSM_HEADS = SSM_DIM // SSM_HEAD_DIM
SSM_GROUPS = 4
SSM_STATE = 128
SSM_CONV = 4
SSM_CONV_DIM = SSM_DIM + 2 * SSM_GROUPS * SSM_STATE
CHUNK = 128
ROPE_BASE = 10000.0
EPS = 1e-6
SPLIT_SIZES = (RET_DIM, RET_DIM, RET_DIM, RET_DIM, SC_DIM, SC_DIM, SC_DIM, SC_DIM, SSM_DIM, SSM_CONV_DIM, SSM_HEADS)
D_PROJ = sum(SPLIT_SIZES)

kernel_name = "hybrid_retention_shortconv_ssd_decoder_step"


def _split_points():
    return [int(p) for p in np.cumsum(SPLIT_SIZES)[:-1]]


def _rmsnorm(x, g):
    x32 = x.astype(jnp.float32)
    y = x32 * lax.rsqrt(jnp.mean(x32 * x32, axis=-1, keepdims=True) + EPS)
    return (y * g.astype(jnp.float32)).astype(x.dtype)


def _head_norm(o, g):
    mu = jnp.mean(o, axis=-1, keepdims=True)
    oc = o - mu
    var = jnp.mean(oc * oc, axis=-1, keepdims=True)
    on = oc * lax.rsqrt(var + EPS)
    return on.reshape(o.shape[0], o.shape[1], -1) * g.astype(jnp.float32)


def _rope(t, pos):
    half = t.shape[-1] // 2
    inv = ROPE_BASE ** (-jnp.arange(half, dtype=jnp.float32) / half)
    ang = pos.astype(jnp.float32)[:, None] * inv[None, :]
    cos = jnp.cos(ang)[None, :, None, :]
    sin = jnp.sin(ang)[None, :, None, :]
    t1, t2 = t[..., :half], t[..., half:]
    return jnp.concatenate([t1 * cos - t2 * sin, t1 * sin + t2 * cos], axis=-1)


def _chunk_len(L):
    return CHUNK if L % CHUNK == 0 else L


def _to_chunks(a, c):
    b, l = a.shape[:2]
    return jnp.swapaxes(a.reshape(b, l // c, c, *a.shape[2:]), 0, 1)


def _from_chunks(a):
    n, b, c = a.shape[:3]
    return jnp.swapaxes(a, 0, 1).reshape(b, n * c, *a.shape[3:])


def _causal_conv(u, buf, w, b):
    width = w.shape[0]
    L = u.shape[1]
    xp = jnp.concatenate([buf, u], axis=1)
    out = b.astype(jnp.float32)
    for j in range(width):
        out = out + xp[:, j:j + L] * w[j].astype(jnp.float32)
    return out, xp[:, xp.shape[1] - (width - 1):]


def _retention(q, k, v, s0):
    L = q.shape[1]
    c = _chunk_len(L)
    lg = jnp.log(1.0 - 2.0 ** (-5.0 - jnp.arange(RET_HEADS, dtype=jnp.float32)))
    i = jnp.arange(c, dtype=jnp.float32)
    diff = i[:, None] - i[None, :]
    dmat = jnp.exp(jnp.where((diff >= 0)[None], diff[None] * lg[:, None, None], -jnp.inf))
    q_dec = jnp.exp((i + 1.0)[:, None] * lg[None, :])
    k_dec = jnp.exp((c - 1.0 - i)[:, None] * lg[None, :])
    chunk_dec = jnp.exp(c * lg)

    def step(s, inp):
        qc, kc, vc = inp
        scores = jnp.einsum('bihd,bjhd->bhij', qc, kc) * dmat[None]
        o = jnp.einsum('bhij,bjhe->bihe', scores, vc)
        o = o + jnp.einsum('bihd,bhde->bihe', qc, s) * q_dec[None, :, :, None]
        s = chunk_dec[None, :, None, None] * s + jnp.einsum('bjhd,bjhe->bhde', kc * k_dec[None, :, :, None], vc)
        return s, o

    s, o = lax.scan(step, s0, (_to_chunks(q, c), _to_chunks(k, c), _to_chunks(v, c)))
    return _from_chunks(o), s


def _ssd(x, dt, A, bm, cm, s0):
    L = x.shape[1]
    c = _chunk_len(L)
    rep = SSM_HEADS // SSM_GROUPS
    bh = jnp.repeat(bm, rep, axis=2)
    ch = jnp.repeat(cm, rep, axis=2)
    a = dt * A.astype(jnp.float32)
    ii = jnp.arange(c)
    mask = (ii[:, None] >= ii[None, :])[None, :, :, None]

    def step(s, inp):
        xc, dtc, ac, bc, cc = inp
        acum = jnp.cumsum(ac, axis=1)
        seg = acum[:, :, None, :] - acum[:, None, :, :]
        lmat = jnp.exp(jnp.where(mask, seg, -jnp.inf))
        cb = jnp.einsum('bihn,bjhn->bijh', cc, bc)
        y = jnp.einsum('bijh,bjhp->bihp', cb * lmat * dtc[:, None, :, :], xc)
        y = y + jnp.einsum('bihn,bhpn->bihp', cc, s) * jnp.exp(acum)[..., None]
        last = acum[:, -1]
        wj = jnp.exp(last[:, None, :] - acum) * dtc
        s = jnp.exp(last)[:, :, None, None] * s + jnp.einsum('bjhn,bjhp->bhpn', bc * wj[..., None], xc)
        return s, y

    xs = (_to_chunks(x, c), _to_chunks(dt, c), _to_chunks(a, c), _to_chunks(bh, c), _to_chunks(ch, c))
    s, y = lax.scan(step, s0, xs)
    return _from_chunks(y), s


def _layer(x, pos, ret_s, sc_buf, ssm_buf, ssm_s, w_in, w_out, g_pre, g_post, g_ret,
           sc_w, sc_b, ssm_w, ssm_b, dt_bias, a_log, d_skip, g_ssm):
    f32 = jnp.float32
    bsz, L, _ = x.shape
    h = _rmsnorm(x, g_pre)
    proj = jnp.einsum('bld,dp->blp', h, w_in).astype(f32)
    q, k, v, g_r, sc_bg, sc_cg, sc_h, g_sc, z, xbc, dt = jnp.split(proj, _split_points(), axis=-1)

    q = _rope(q.reshape(bsz, L, RET_HEADS, RET_HEAD_DIM), pos)
    k = _rope(k.reshape(bsz, L, RET_HEADS, RET_HEAD_DIM), pos) * (RET_HEAD_DIM ** -0.5)
    v = v.reshape(bsz, L, RET_HEADS, RET_HEAD_DIM)
    o_ret, ret_new = _retention(q, k, v, ret_s.astype(f32))
    o_ret = _head_norm(o_ret, g_ret) * jax.nn.silu(g_r)

    conv, sc_new = _causal_conv(sc_cg * sc_h, sc_buf.astype(f32), sc_w, sc_b)
    o_sc = sc_bg * conv * jax.nn.silu(g_sc)

    xbc_c, ssm_conv_new = _causal_conv(xbc, ssm_buf.astype(f32), ssm_w, ssm_b)
    xbc_c = jax.nn.silu(xbc_c)
    xs, bm, cm = jnp.split(xbc_c, [SSM_DIM, SSM_DIM + SSM_GROUPS * SSM_STATE], axis=-1)
    xs = xs.reshape(bsz, L, SSM_HEADS, SSM_HEAD_DIM)
    bm = bm.reshape(bsz, L, SSM_GROUPS, SSM_STATE)
    cm = cm.reshape(bsz, L, SSM_GROUPS, SSM_STATE)
    dt = jax.nn.softplus(dt + dt_bias.astype(f32))
    A = -jnp.exp(a_log.astype(f32))
    y, ssm_new = _ssd(xs, dt, A, bm, cm, ssm_s.astype(f32))
    y = y + d_skip.astype(f32)[:, None] * xs
    o_ssm = _rmsnorm(y.reshape(bsz, L, SSM_DIM) * jax.nn.silu(z), g_ssm)

    mix = jnp.concatenate([o_ret, o_sc, o_ssm], axis=-1).astype(x.dtype)
    out = _rmsnorm(jnp.einsum('blm,md->bld', mix, w_out), g_post)
    dt_out = x.dtype
    return x + out, (ret_new.astype(dt_out), sc_new.astype(dt_out), ssm_conv_new.astype(dt_out), ssm_new.astype(dt_out))


def setup_inputs(seed: int = 0) -> dict:
    key = jax.random.key(seed)
    ks = jax.random.split(key, 20)
    f32 = jnp.float32
    nrm = lambda k, s: jax.random.normal(k, s, f32)
    dt0 = jnp.exp(jax.random.uniform(ks[14], (DEPTH, SSM_HEADS), f32, np.log(1e-3), np.log(1e-1)))
    return {
        "x_prompt": nrm(ks[0], (BATCH, SEQ, D_MODEL)),
        "x_sample": nrm(ks[1], (DEC_BATCH, DEC_SEQ, D_MODEL)),
        "state_ret": 0.5 * nrm(ks[2], (DEPTH, DEC_BATCH, RET_HEADS, RET_HEAD_DIM, RET_HEAD_DIM)),
        "state_sconv": nrm(ks[3], (DEPTH, DEC_BATCH, SC_WIDTH - 1, SC_DIM)),
        "state_ssm_conv": nrm(ks[4], (DEPTH, DEC_BATCH, SSM_CONV - 1, SSM_CONV_DIM)),
        "state_ssm": 0.1 * nrm(ks[5], (DEPTH, DEC_BATCH, SSM_HEADS, SSM_HEAD_DIM, SSM_STATE)),
        "w_in": nrm(ks[6], (DEPTH, D_MODEL, D_PROJ)) * D_MODEL ** -0.5,
        "w_out": nrm(ks[7], (DEPTH, D_MIX, D_MODEL)) * D_MIX ** -0.5,
        "norm_pre": 1.0 + 0.05 * nrm(ks[8], (DEPTH, D_MODEL)),
        "norm_post": 1.0 + 0.05 * nrm(ks[9], (DEPTH, D_MODEL)),
        "ret_norm": 1.0 + 0.05 * nrm(ks[10], (DEPTH, RET_DIM)),
        "sc_conv_w": nrm(ks[11], (DEPTH, SC_WIDTH, SC_DIM)) * SC_WIDTH ** -0.5,
        "sc_conv_b": 0.02 * nrm(ks[12], (DEPTH, SC_DIM)),
        "ssm_conv_w": nrm(ks[13], (DEPTH, SSM_CONV, SSM_CONV_DIM)) * SSM_CONV ** -0.5,
        "ssm_conv_b": 0.02 * nrm(ks[15], (DEPTH, SSM_CONV_DIM)),
        "ssm_dt_bias": dt0 + jnp.log(-jnp.expm1(-dt0)),
        "ssm_a_log": jnp.log(jax.random.uniform(ks[16], (DEPTH, SSM_HEADS), f32, 1.0, 16.0)),
        "ssm_d": 1.0 + 0.1 * nrm(ks[17], (DEPTH, SSM_HEADS)),
        "ssm_norm": 1.0 + 0.05 * nrm(ks[18], (DEPTH, SSM_DIM)),
    }


def reference(x_prompt, x_sample, state_ret, state_sconv, state_ssm_conv, state_ssm,
              w_in, w_out, norm_pre, norm_post, ret_norm, sc_conv_w, sc_conv_b,
              ssm_conv_w, ssm_conv_b, ssm_dt_bias, ssm_a_log, ssm_d, ssm_norm):
    f32 = jnp.float32
    bp, lp = x_prompt.shape[0], x_prompt.shape[1]
    ls = x_sample.shape[1]
    pos_p = jnp.arange(lp)
    pos_s = PAST_LEN + jnp.arange(ls)
    z_ret = jnp.zeros((bp, RET_HEADS, RET_HEAD_DIM, RET_HEAD_DIM), f32)
    z_sc = jnp.zeros((bp, SC_WIDTH - 1, SC_DIM), f32)
    z_sconv = jnp.zeros((bp, SSM_CONV - 1, SSM_CONV_DIM), f32)
    z_ssm = jnp.zeros((bp, SSM_HEADS, SSM_HEAD_DIM, SSM_STATE), f32)

    yp, ys = x_prompt, x_sample
    np_ret, np_sc, np_sconv, np_ssm = [], [], [], []
    ns_ret, ns_sc, ns_sconv, ns_ssm = [], [], [], []
    for l in range(DEPTH):
        lw = (w_in[l], w_out[l], norm_pre[l], norm_post[l], ret_norm[l], sc_conv_w[l], sc_conv_b[l],
              ssm_conv_w[l], ssm_conv_b[l], ssm_dt_bias[l], ssm_a_log[l], ssm_d[l], ssm_norm[l])
        yp, (r, c, sc, s) = _layer(yp, pos_p, z_ret, z_sc, z_sconv, z_ssm, *lw)
        np_ret.append(r); np_sc.append(c); np_sconv.append(sc); np_ssm.append(s)
        ys, (r, c, sc, s) = _layer(ys, pos_s, state_ret[l], state_sconv[l], state_ssm_conv[l], state_ssm[l], *lw)
        ns_ret.append(r); ns_sc.append(c); ns_sconv.append(sc); ns_ssm.append(s)

    return (yp, ys,
            jnp.stack(np_ret), jnp.stack(np_sc), jnp.stack(np_sconv), jnp.stack(np_ssm),
            jnp.stack(ns_ret), jnp.stack(ns_sc), jnp.stack(ns_sconv), jnp.stack(ns_ssm))
```

```python
import functools

import jax
import jax.numpy as jnp
import numpy as np
from jax import lax
from jax.experimental import pallas as pl
from jax.experimental.pallas import tpu as pltpu

F32 = jnp.float32
BF16 = jnp.bfloat16

D_MODEL = 2048
D_MIX = 2 * D_MODEL
RET_HEADS = 8
RET_DIM = 1024
HEAD_DIM = 128
SC_DIM = 1024
SC_WIDTH = 3
SSM_DIM = 2048
SSM_HEAD_DIM = 64
SSM_HEADS = 32
SSM_GROUPS = 4
SSM_STATE = 128
SSM_CONV = 4
SSM_CONV_DIM = SSM_DIM + 2 * SSM_GROUPS * SSM_STATE
CHUNK = 128
ROPE_BASE = 10000.0
EPS = 1e-6
PAST_LEN = 16384

OFF_Q, OFF_K, OFF_V, OFF_GR = 0, 1024, 2048, 3072
OFF_BG, OFF_CG, OFF_SH, OFF_GS = 4096, 5120, 6144, 7168
OFF_Z, OFF_X, OFF_B, OFF_C, OFF_DT = 8192, 10240, 12288, 12800, 13312
N_MAIN = OFF_DT

LANES = 128
SUBLANES = 8
V7X_VMEM_LIMIT = 56 * 1024 * 1024
PAIR = 2 * SSM_HEAD_DIM
HEADS_PER_GROUP = SSM_HEADS // SSM_GROUPS
PAIRS_PER_GROUP = HEADS_PER_GROUP // 2
N_PAIRS = SSM_HEADS // 2


def _params(*sem):
    return pltpu.CompilerParams(dimension_semantics=sem, vmem_limit_bytes=V7X_VMEM_LIMIT)


def _silu(x):
    return x * jax.nn.sigmoid(x)


def _softplus(x):
    return jnp.maximum(x, 0.0) + jnp.log1p(jnp.exp(-jnp.abs(x)))


def _dot(a, b):
    return jnp.dot(a, b, preferred_element_type=F32)


def _dot_nt(a, b):
    return lax.dot_general(a, b, (((1,), (1,)), ((), ())), preferred_element_type=F32)


def _prenorm_kernel(x_ref, g_ref, o_ref):
    x = x_ref[...]
    ms = jnp.mean(x * x, axis=-1, keepdims=True)
    o_ref[...] = (x * lax.rsqrt(ms + EPS) * g_ref[...]).astype(o_ref.dtype)


def _prenorm(x, g, tm):
    m, d = x.shape
    return pl.pallas_call(
        _prenorm_kernel,
        out_shape=jax.ShapeDtypeStruct((m, d), BF16),
        grid=(m // tm,),
        in_specs=[pl.BlockSpec((tm, d), lambda i: (i, 0)), pl.BlockSpec((1, d), lambda i: (0, 0))],
        out_specs=pl.BlockSpec((tm, d), lambda i: (i, 0)),
        compiler_params=_params("parallel"),
        name="prenorm",
    )(x, g)


def _mm_kernel(x_ref, w_ref, o_ref):
    o_ref[...] = _dot(x_ref[...], w_ref[...])


def _matmul(x, w, layer, n_cols, tm, tn, name):
    m, k = x.shape
    return pl.pallas_call(
        _mm_kernel,
        out_shape=jax.ShapeDtypeStruct((m, n_cols), F32),
        grid=(m // tm, n_cols // tn),
        in_specs=[pl.BlockSpec((tm, k), lambda i, j: (i, 0)),
                  pl.BlockSpec((None, k, tn), lambda i, j: (layer, 0, j))],
        out_specs=pl.BlockSpec((tm, tn), lambda i, j: (i, j)),
        compiler_params=_params("parallel", "parallel"),
        name=name,
    )(x, w)


def _outproj_kernel(oret_ref, osc_ref, ssm_ref, x_ref, w_ref, gssm_ref, gpost_ref, gnext_ref, y_ref, hn_ref):
    ypre = ssm_ref[...]
    ms = jnp.mean(ypre * ypre, axis=-1, keepdims=True)
    ossm = (ypre * lax.rsqrt(ms + EPS) * gssm_ref[...]).astype(BF16)
    acc = _dot(oret_ref[...], w_ref[0:RET_DIM, :])
    acc = acc + _dot(osc_ref[...], w_ref[RET_DIM:RET_DIM + SC_DIM, :])
    acc = acc + _dot(ossm, w_ref[RET_DIM + SC_DIM:, :])
    ms2 = jnp.mean(acc * acc, axis=-1, keepdims=True)
    y = x_ref[...] + acc * lax.rsqrt(ms2 + EPS) * gpost_ref[...]
    y_ref[...] = y
    ms3 = jnp.mean(y * y, axis=-1, keepdims=True)
    hn_ref[...] = (y * lax.rsqrt(ms3 + EPS) * gnext_ref[...]).astype(hn_ref.dtype)


def _outproj(oret, osc, ssm_pre, x, w_out, layer, g_ssm, g_post, g_next, tm):
    m = x.shape[0]
    row = lambda width: pl.BlockSpec((tm, width), lambda i: (i, 0))
    vec = lambda width: pl.BlockSpec((1, width), lambda i: (0, 0))
    return pl.pallas_call(
        _outproj_kernel,
        out_shape=(jax.ShapeDtypeStruct((m, D_MODEL), F32), jax.ShapeDtypeStruct((m, D_MODEL), BF16)),
        grid=(m // tm,),
        in_specs=[row(RET_DIM), row(SC_DIM), row(SSM_DIM), row(D_MODEL),
                  pl.BlockSpec((None, D_MIX, D_MODEL), lambda i: (layer, 0, 0), pipeline_mode=pl.Buffered(1)),
                  vec(SSM_DIM), vec(D_MODEL), vec(D_MODEL)],
        out_specs=(row(D_MODEL), row(D_MODEL)),
        compiler_params=_params("parallel"),
        name="outproj",
    )(oret, osc, ssm_pre, x, w_out, g_ssm, g_post, g_next)


def _rope(t, cos2, sin2):
    return t * cos2 + pltpu.roll(t, HEAD_DIM // 2, 1) * sin2


def _head_norm_gate(o, g_row, gate):
    mu = jnp.mean(o, axis=-1, keepdims=True)
    oc = o - mu
    var = jnp.mean(oc * oc, axis=-1, keepdims=True)
    return oc * lax.rsqrt(var + EPS) * g_row * _silu(gate)


def _shift_rows(tail, u, s):
    ext = jnp.concatenate([tail, u], axis=0)
    return ext[SUBLANES - s:SUBLANES - s + u.shape[0]]


def _causal_conv_rows(tail, u, w_ref, b_ref):
    width = w_ref.shape[0]
    out = b_ref[...]
    for j in range(width - 1):
        out = out + _shift_rows(tail, u, width - 1 - j) * w_ref[j:j + 1, :]
    return out + u * w_ref[width - 1:width, :]


def _ret_prompt_kernel(q_ref, k_ref, v_ref, g_ref, cos_ref, sin_ref, dmat_ref, qdec_ref, kdec_ref, cdec_ref,
                       gret_ref, o_ref, st_ref, s_scr):
    n_chunks = q_ref.shape[0] // CHUNK
    heads = q_ref.shape[1] // HEAD_DIM
    for hh in range(heads):
        sl = slice(hh * HEAD_DIM, (hh + 1) * HEAD_DIM)
        s_scr[...] = jnp.zeros_like(s_scr)

        def chunk(n, carry, hh=hh, sl=sl):
            rows = pl.ds(pl.multiple_of(n * CHUNK, CHUNK), CHUNK)
            cos2 = cos_ref[rows, :]
            sin2 = sin_ref[rows, :]
            qr = _rope(q_ref[rows, sl], cos2, sin2)
            kr = _rope(k_ref[rows, sl], cos2, sin2) * (HEAD_DIM ** -0.5)
            qb = qr.astype(BF16)
            vb = v_ref[rows, sl].astype(BF16)
            scores = _dot_nt(qb, kr.astype(BF16)) * dmat_ref[hh]
            s_old = s_scr[...]
            o = _dot(scores.astype(BF16), vb) + _dot(qb, s_old.astype(BF16)) * qdec_ref[hh]
            kd_t = jnp.transpose(kr * kdec_ref[hh]).astype(BF16)
            s_scr[...] = cdec_ref[hh] * s_old + _dot(kd_t, vb)
            o_ref[rows, sl] = _head_norm_gate(o, gret_ref[:, sl], g_ref[rows, sl]).astype(o_ref.dtype)
            return carry

        lax.fori_loop(0, n_chunks, chunk, 0)
        st_ref[0, hh] = s_scr[...]


def _ret_prompt(proj, batch, seq, cos2, sin2, tabs, g_ret):
    hp = 2
    width = hp * HEAD_DIM
    col = lambda off: pl.BlockSpec((seq, width), lambda b, h, off=off: (b, off // width + h))
    tab = pl.BlockSpec((hp, CHUNK, CHUNK), lambda b, h: (h, 0, 0))
    full = pl.BlockSpec((seq, HEAD_DIM), lambda b, h: (0, 0))
    return pl.pallas_call(
        _ret_prompt_kernel,
        out_shape=(jax.ShapeDtypeStruct((batch * seq, RET_DIM), BF16),
                   jax.ShapeDtypeStruct((batch, RET_HEADS, HEAD_DIM, HEAD_DIM), F32)),
        grid=(batch, RET_HEADS // hp),
        in_specs=[col(OFF_Q), col(OFF_K), col(OFF_V), col(OFF_GR), full, full, tab, tab, tab, tab,
                  pl.BlockSpec((1, width), lambda b, h: (0, h))],
        out_specs=(pl.BlockSpec((seq, width), lambda b, h: (b, h)),
                   pl.BlockSpec((1, hp, HEAD_DIM, HEAD_DIM), lambda b, h: (b, h, 0, 0))),
        scratch_shapes=[pltpu.VMEM((HEAD_DIM, HEAD_DIM), F32)],
        compiler_params=_params("parallel", "parallel"),
        name="ret_prompt",
    )(proj, proj, proj, proj, cos2, sin2, *tabs, g_ret)


def _ret_decode_kernel(q_ref, k_ref, v_ref, g_ref, cos_ref, sin_ref, gam_ref, gret_ref, s_ref, o_ref, sn_ref):
    bt = q_ref.shape[0]
    cos2 = cos_ref[...]
    sin2 = sin_ref[...]
    pad = jnp.zeros((LANES - 2 * bt, HEAD_DIM), F32)
    for h in range(RET_HEADS):
        sl = slice(h * HEAD_DIM, (h + 1) * HEAD_DIM)
        gam = gam_ref[h][0:1, :]
        qr = _rope(q_ref[:, sl], cos2, sin2)
        kr = _rope(k_ref[:, sl], cos2, sin2) * (HEAD_DIM ** -0.5)
        v = v_ref[:, sl]
        qk = jnp.sum(qr * kr, axis=-1, keepdims=True)
        cols = jnp.transpose(jnp.concatenate([qr, kr, pad], axis=0))
        q_s = []
        for j in range(bt):
            s_old = s_ref[j, h]
            q_s.append(jnp.sum(s_old * cols[:, j:j + 1], axis=0, keepdims=True))
            sn_ref[j, h] = gam * s_old + cols[:, bt + j:bt + j + 1] * v[j:j + 1, :]
        o = qk * v + jnp.concatenate(q_s, axis=0) * gam
        o_ref[:, sl] = _head_norm_gate(o, gret_ref[:, sl], g_ref[:, sl]).astype(o_ref.dtype)


def _ret_decode(proj, state, cos2, sin2, gam_tab, g_ret, bt):
    nb = proj.shape[0]
    col = lambda off: pl.BlockSpec((bt, RET_DIM), lambda i, off=off: (i, off // RET_DIM))
    st = pl.BlockSpec((bt, RET_HEADS, HEAD_DIM, HEAD_DIM), lambda i: (i, 0, 0, 0))
    return pl.pallas_call(
        _ret_decode_kernel,
        out_shape=(jax.ShapeDtypeStruct((nb, RET_DIM), BF16), jax.ShapeDtypeStruct(state.shape, F32)),
        grid=(nb // bt,),
        in_specs=[col(OFF_Q), col(OFF_K), col(OFF_V), col(OFF_GR),
                  pl.BlockSpec((1, HEAD_DIM), lambda i: (0, 0)), pl.BlockSpec((1, HEAD_DIM), lambda i: (0, 0)),
                  pl.BlockSpec((RET_HEADS, SUBLANES, LANES), lambda i: (0, 0, 0)),
                  pl.BlockSpec((1, RET_DIM), lambda i: (0, 0)), st],
        out_specs=(pl.BlockSpec((bt, RET_DIM), lambda i: (i, 0)), st),
        compiler_params=_params("parallel"),
        name="ret_decode",
    )(proj, proj, proj, proj, cos2, sin2, gam_tab, g_ret, state)


def _sc_prompt_kernel(bg_ref, cg_ref, sh_ref, gs_ref, w_ref, b_ref, o_ref, st_ref):
    seq, width = bg_ref.shape

    def chunk(n, tail):
        rows = pl.ds(pl.multiple_of(n * CHUNK, CHUNK), CHUNK)
        u = cg_ref[rows, :] * sh_ref[rows, :]
        conv = _causal_conv_rows(tail, u, w_ref, b_ref)
        o_ref[rows, :] = (bg_ref[rows, :] * conv * _silu(gs_ref[rows, :])).astype(o_ref.dtype)
        return u[CHUNK - SUBLANES:, :]

    tail = lax.fori_loop(0, seq // CHUNK, chunk, jnp.zeros((SUBLANES, width), F32))
    st_ref[0] = tail[SUBLANES - (SC_WIDTH - 1):, :]


def _sc_prompt(proj, batch, seq, w, b):
    width = 256
    col = lambda off: pl.BlockSpec((seq, width), lambda bi, c, off=off: (bi, off // width + c))
    return pl.pallas_call(
        _sc_prompt_kernel,
        out_shape=(jax.ShapeDtypeStruct((batch * seq, SC_DIM), BF16),
                   jax.ShapeDtypeStruct((batch, SC_WIDTH - 1, SC_DIM), F32)),
        grid=(batch, SC_DIM // width),
        in_specs=[col(OFF_BG), col(OFF_CG), col(OFF_SH), col(OFF_GS),
                  pl.BlockSpec((SC_WIDTH, width), lambda bi, c: (0, c)),
                  pl.BlockSpec((1, width), lambda bi, c: (0, c))],
        out_specs=(pl.BlockSpec((seq, width), lambda bi, c: (bi, c)),
                   pl.BlockSpec((1, SC_WIDTH - 1, width), lambda bi, c: (bi, 0, c))),
        compiler_params=_params("parallel", "parallel"),
        name="sc_prompt",
    )(proj, proj, proj, proj, w, b)


def _sc_decode_kernel(bg_ref, cg_ref, sh_ref, gs_ref, st_ref, w_ref, b_ref, o_ref, stn_ref):
    u = cg_ref[...] * sh_ref[...]
    r0 = st_ref[:, 0:SC_DIM]
    r1 = st_ref[:, SC_DIM:]
    conv = b_ref[...] + r0 * w_ref[0:1, :] + r1 * w_ref[1:2, :] + u * w_ref[2:3, :]
    o_ref[...] = (bg_ref[...] * conv * _silu(gs_ref[...])).astype(o_ref.dtype)
    stn_ref[:, 0:SC_DIM] = r1
    stn_ref[:, SC_DIM:] = u


def _sc_decode(proj, state2d, w, b):
    nb = proj.shape[0]
    col = lambda off: pl.BlockSpec((nb, SC_DIM), lambda i, off=off: (0, off // SC_DIM))
    whole = lambda shape: pl.BlockSpec(shape, lambda i: (0,) * len(shape))
    return pl.pallas_call(
        _sc_decode_kernel,
        out_shape=(jax.ShapeDtypeStruct((nb, SC_DIM), BF16), jax.ShapeDtypeStruct(state2d.shape, F32)),
        grid=(1,),
        in_specs=[col(OFF_BG), col(OFF_CG), col(OFF_SH), col(OFF_GS), whole(state2d.shape),
                  whole((SC_WIDTH, SC_DIM)), whole((1, SC_DIM))],
        out_specs=(whole((nb, SC_DIM)), whole(state2d.shape)),
        compiler_params=_params("arbitrary"),
        name="sc_decode",
    )(proj, proj, proj, proj, state2d, w, b)


def _chunk_cumsum(a, row):
    s = 1
    while s < CHUNK:
        a = a + jnp.where(row >= s, pltpu.roll(a, s, 0), 0.0)
        s *= 2
    return a


def _ssd_prompt_kernel(z_ref, x_ref, b_ref, c_ref, dtr_ref, dtb_ref, alog_ref, dsk_ref,
                       cwx_ref, cwb_ref, cwc_ref, cbx_ref, cbb_ref, cbc_ref,
                       y_ref, stx_ref, stb_ref, stc_ref, sst_ref, s_scr):
    seq = z_ref.shape[0]
    row = lax.broadcasted_iota(jnp.int32, (CHUNK, CHUNK), 0)
    lane = lax.broadcasted_iota(jnp.int32, (CHUNK, CHUNK), 1)
    causal = row >= lane
    low_lanes = lane < SSM_HEAD_DIM
    low_rows = row < SSM_HEAD_DIM
    a_neg = -jnp.exp(alog_ref[0])
    dt_bias = dtb_ref[0]
    s_scr[...] = jnp.zeros_like(s_scr)

    def chunk(n, tails):
        tx, tb, tc = tails
        rows = pl.ds(pl.multiple_of(n * CHUNK, CHUNK), CHUNK)
        x_raw = x_ref[rows, :]
        b_raw = b_ref[rows, :]
        c_raw = c_ref[rows, :]
        xc = _silu(_causal_conv_rows(tx, x_raw, cwx_ref, cbx_ref))
        bb = _silu(_causal_conv_rows(tb, b_raw, cwb_ref, cbb_ref)).astype(BF16)
        cb16 = _silu(_causal_conv_rows(tc, c_raw, cwc_ref, cbc_ref)).astype(BF16)
        dt = _softplus(dtr_ref[rows, :] + dt_bias)
        acum = _chunk_cumsum(dt * a_neg, row)
        acum_t = jnp.transpose(acum)
        dt_t = jnp.transpose(dt)
        cb = _dot_nt(cb16, bb)
        for pp in range(PAIRS_PER_GROUP):
            sl = slice(pp * PAIR, (pp + 1) * PAIR)
            x_pair = xc[:, sl]
            m, e_col, w_row, dec = [], [], [], []
            for hh in range(2):
                c = 2 * pp + hh
                a_col = jnp.broadcast_to(acum[:, c:c + 1], (CHUNK, CHUNK))
                a_row = acum_t[c:c + 1, :]
                dt_row = dt_t[c:c + 1, :]
                lmat = jnp.exp(jnp.where(causal, a_col - a_row, -jnp.inf))
                m.append((cb * lmat * dt_row).astype(BF16))
                e_col.append(jnp.exp(a_col))
                last = a_col[CHUNK - 1:CHUNK, :]
                w_row.append(jnp.exp(last - a_row) * dt_row)
                dec.append(jnp.exp(last))
            x_lo = jnp.where(low_lanes, x_pair, 0.0).astype(BF16)
            x_hi = jnp.where(low_lanes, 0.0, x_pair).astype(BF16)
            s_old = s_scr[pp]
            y = _dot(m[0], x_lo) + _dot(m[1], x_hi)
            y = y + _dot_nt(cb16, s_old.astype(BF16)) * jnp.where(low_lanes, e_col[0], e_col[1])
            y = y + dsk_ref[0][:, sl] * x_pair
            y_ref[rows, sl] = y * _silu(z_ref[rows, sl])
            xw = (jnp.transpose(x_pair) * jnp.where(low_rows, w_row[0], w_row[1])).astype(BF16)
            s_scr[pp] = jnp.where(low_rows, dec[0], dec[1]) * s_old + _dot(xw, bb)
        cut = CHUNK - SUBLANES
        return x_raw[cut:, :], b_raw[cut:, :], c_raw[cut:, :]

    zeros = lambda r: jnp.zeros((SUBLANES, r.shape[1]), F32)
    tx, tb, tc = lax.fori_loop(0, seq // CHUNK, chunk, (zeros(x_ref), zeros(b_ref), zeros(c_ref)))
    keep = SUBLANES - (SSM_CONV - 1)
    stx_ref[0] = tx[keep:, :]
    stb_ref[0] = tb[keep:, :]
    stc_ref[0] = tc[keep:, :]
    for pp in range(PAIRS_PER_GROUP):
        for hh in range(2):
            sst_ref[0, 2 * pp + hh] = s_scr[pp, hh * SSM_HEAD_DIM:(hh + 1) * SSM_HEAD_DIM, :]


def _ssd_prompt(proj, dt_raw, batch, seq, dtb, alog, dsk, conv_w, conv_b):
    gw = SSM_DIM // SSM_GROUPS
    n = SSM_STATE
    bcol = lambda off, width: pl.BlockSpec((seq, width), lambda b, g, off=off, width=width: (b, off // width + g))
    grp = lambda width: pl.BlockSpec((1, 1, width), lambda b, g: (g, 0, 0))
    cw = lambda off, width: pl.BlockSpec((SSM_CONV, width), lambda b, g, off=off, width=width: (0, off // width + g))
    cbias = lambda off, width: pl.BlockSpec((1, width), lambda b, g, off=off, width=width: (0, off // width + g))
    st = lambda width: pl.BlockSpec((1, SSM_CONV - 1, width), lambda b, g: (b, 0, g))
    return pl.pallas_call(
        _ssd_prompt_kernel,
        out_shape=(jax.ShapeDtypeStruct((batch * seq, SSM_DIM), F32),
                   jax.ShapeDtypeStruct((batch, SSM_CONV - 1, SSM_DIM), F32),
                   jax.ShapeDtypeStruct((batch, SSM_CONV - 1, SSM_GROUPS * n), F32),
                   jax.ShapeDtypeStruct((batch, SSM_CONV - 1, SSM_GROUPS * n), F32),
                   jax.ShapeDtypeStruct((batch, SSM_HEADS, SSM_HEAD_DIM, n), F32)),
        grid=(batch, SSM_GROUPS),
        in_specs=[bcol(OFF_Z, gw), bcol(OFF_X, gw), bcol(OFF_B, n), bcol(OFF_C, n),
                  pl.BlockSpec((seq, LANES), lambda b, g: (b, g)),
                  grp(LANES), grp(LANES), grp(gw),
                  cw(0, gw), cw(SSM_DIM, n), cw(SSM_DIM + SSM_GROUPS * n, n),
                  cbias(0, gw), cbias(SSM_DIM, n), cbias(SSM_DIM + SSM_GROUPS * n, n)],
        out_specs=(pl.BlockSpec((seq, gw), lambda b, g: (b, g)), st(gw), st(n), st(n),
                   pl.BlockSpec((1, HEADS_PER_GROUP, SSM_HEAD_DIM, n), lambda b, g: (b, g, 0, 0))),
        scratch_shapes=[pltpu.VMEM((PAIRS_PER_GROUP, PAIR, n), F32)],
        compiler_params=_params("parallel", "parallel"),
        name="ssd_prompt",
    )(proj, proj, proj, proj, dt_raw, dtb, alog, dsk, conv_w, conv_w, conv_w, conv_b, conv_b, conv_b)


def _ssd_decode_kernel(z_ref, x_ref, bc_ref, dtr_ref, cst_ref, s_ref, cw_ref, cb_ref, dtb_ref, alog_ref, dsk_ref,
                       y_ref, cstn_ref, sn_ref):
    bt = z_ref.shape[0]
    xr = SSM_DIM // LANES
    gr = SSM_GROUPS * SSM_STATE // LANES
    u = jnp.concatenate([x_ref[...], bc_ref[...]], axis=1)
    conv = cb_ref[...]
    for j in range(SSM_CONV - 1):
        conv = conv + cst_ref[:, j] * cw_ref[j]
        cstn_ref[:, j] = u if j == SSM_CONV - 2 else cst_ref[:, j + 1]
    act = _silu(conv + u * cw_ref[SSM_CONV - 1])
    xs = act[:, 0:xr]
    bm = act[:, xr:xr + gr]
    cm = act[:, xr + gr:]
    dt = _softplus(dtr_ref[...] + dtb_ref[...])
    ea = jnp.exp(dt * -jnp.exp(alog_ref[...]))
    xdt = xs * dt
    row = lax.broadcasted_iota(jnp.int32, (LANES, LANES), 0)
    lane = lax.broadcasted_iota(jnp.int32, (LANES, LANES), 1)
    low_rows = row < SSM_HEAD_DIM
    row16 = lax.broadcasted_iota(jnp.int32, (xr, LANES), 0)
    pad = jnp.zeros((LANES - 2 * xr, LANES), F32)
    for j in range(bt):
        cols = jnp.transpose(jnp.concatenate([xdt[j], ea[j], pad], axis=0))
        y_t = jnp.zeros((LANES, LANES), F32)
        cbt = jnp.zeros((xr, LANES), F32)
        for g in range(SSM_GROUPS):
            cb_g = jnp.sum(cm[j, g:g + 1, :] * bm[j, g:g + 1, :], axis=-1, keepdims=True)
            cbt = jnp.where(row16 // PAIRS_PER_GROUP == g, cb_g, cbt)
        for pp in range(N_PAIRS):
            g = pp // PAIRS_PER_GROUP
            s_old = s_ref[j, 2 * pp:2 * pp + 2].reshape(PAIR, SSM_STATE)
            y_col = jnp.sum(s_old * cm[j, g:g + 1, :], axis=-1, keepdims=True)
            y_t = jnp.where(lane == pp, y_col, y_t)
            s_new = cols[:, xr + pp:xr + pp + 1] * s_old + cols[:, pp:pp + 1] * bm[j, g:g + 1, :]
            sn_ref[j, 2 * pp:2 * pp + 2] = s_new.reshape(2, SSM_HEAD_DIM, SSM_STATE)
        y_inter = jnp.transpose(y_t)[0:xr, :]
        y = cbt * xdt[j] + y_inter * ea[j] + dsk_ref[0] * xs[j]
        y_ref[j] = y * _silu(z_ref[j])


def _ssd_decode(proj3, dt_raw3, cstate4, state, conv_w3, conv_b3, dtb3, alog3, dsk3, bt):
    nb = proj3.shape[0]
    xr = SSM_DIM // LANES
    cr = SSM_CONV_DIM // LANES
    whole = lambda shape: pl.BlockSpec(shape, lambda i: (0,) * len(shape))
    st = pl.BlockSpec((bt, SSM_HEADS, SSM_HEAD_DIM, SSM_STATE), lambda i: (i, 0, 0, 0))
    cst = pl.BlockSpec((bt, SSM_CONV - 1, cr, LANES), lambda i: (i, 0, 0, 0))
    return pl.pallas_call(
        _ssd_decode_kernel,
        out_shape=(jax.ShapeDtypeStruct((nb, xr, LANES), F32), jax.ShapeDtypeStruct(cstate4.shape, F32),
                   jax.ShapeDtypeStruct(state.shape, F32)),
        grid=(nb // bt,),
        in_specs=[pl.BlockSpec((bt, xr, LANES), lambda i: (i, OFF_Z // SSM_DIM, 0)),
                  pl.BlockSpec((bt, xr, LANES), lambda i: (i, OFF_X // SSM_DIM, 0)),
                  pl.BlockSpec((bt, SUBLANES, LANES), lambda i: (i, OFF_B // (SUBLANES * LANES), 0)),
                  pl.BlockSpec((bt, xr, LANES), lambda i: (i, 0, 0)),
                  cst, st, whole((SSM_CONV, cr, LANES)), whole((1, cr, LANES)),
                  whole((1, xr, LANES)), whole((1, xr, LANES)), whole((1, xr, LANES))],
        out_specs=(pl.BlockSpec((bt, xr, LANES), lambda i: (i, 0, 0)), cst, st),
        compiler_params=_params("parallel"),
        name="ssd_decode",
    )(proj3, proj3, proj3, dt_raw3, cstate4, state, conv_w3, conv_b3, dtb3, alog3, dsk3)


def _rope_tables(pos):
    half = HEAD_DIM // 2
    inv = ROPE_BASE ** (-jnp.arange(half, dtype=F32) / half)
    ang = pos.astype(F32)[:, None] * inv[None, :]
    cos, sin = jnp.cos(ang), jnp.sin(ang)
    return jnp.concatenate([cos, cos], axis=-1), jnp.concatenate([-sin, sin], axis=-1)


def _retention_tables():
    lg = jnp.log(1.0 - 2.0 ** (-5.0 - jnp.arange(RET_HEADS, dtype=F32)))
    i = jnp.arange(CHUNK, dtype=F32)
    diff = i[:, None] - i[None, :]
    dmat = jnp.exp(jnp.where((diff >= 0)[None], diff[None] * lg[:, None, None], -jnp.inf))
    full = lambda t: jnp.broadcast_to(t, (RET_HEADS, CHUNK, CHUNK))
    q_dec = full(jnp.exp((i + 1.0)[None, :, None] * lg[:, None, None]))
    k_dec = full(jnp.exp((CHUNK - 1.0 - i)[None, :, None] * lg[:, None, None]))
    c_dec = full(jnp.exp(CHUNK * lg)[:, None, None])
    gam = jnp.broadcast_to(jnp.exp(lg)[:, None, None], (RET_HEADS, SUBLANES, LANES))
    return (dmat, q_dec, k_dec, c_dec), gam


def _per_group(v, width):
    v = v.astype(F32).reshape(SSM_GROUPS, 1, HEADS_PER_GROUP)
    return jnp.pad(v, ((0, 0), (0, 0), (0, width - HEADS_PER_GROUP)))


def _per_lane(v):
    return jnp.repeat(v.astype(F32), SSM_HEAD_DIM)[None, :]


def kernel(x_prompt, x_sample, state_ret, state_sconv, state_ssm_conv, state_ssm, w_in, w_out, norm_pre, norm_post,
           ret_norm, sc_conv_w, sc_conv_b, ssm_conv_w, ssm_conv_b, ssm_dt_bias, ssm_a_log, ssm_d, ssm_norm):
    batch, seq, _ = x_prompt.shape
    nb = x_sample.shape[0]
    depth = w_in.shape[0]
    mp = batch * seq
    xr = SSM_DIM // LANES
    cr = SSM_CONV_DIM // LANES

    cos_p, sin_p = _rope_tables(jnp.arange(seq))
    cos_s, sin_s = _rope_tables(PAST_LEN + jnp.arange(1))
    ret_tabs, gam_tab = _retention_tables()

    w_in16 = w_in.astype(BF16)
    w_out16 = w_out.astype(BF16)
    w_dt = w_in[:, :, OFF_DT:]
    w_dt_grp = jnp.pad(w_dt.reshape(depth, D_MODEL, SSM_GROUPS, HEADS_PER_GROUP),
                       ((0, 0), (0, 0), (0, 0), (0, LANES - HEADS_PER_GROUP)))
    w_dt_grp = w_dt_grp.reshape(depth, D_MODEL, SSM_GROUPS * LANES).astype(BF16)
    w_dt_lane = jnp.repeat(w_dt, SSM_HEAD_DIM, axis=2).astype(BF16)

    tm_p = min(1024, mp)
    tm_o = min(256, mp)
    xp = x_prompt.reshape(mp, D_MODEL)
    xs = x_sample.reshape(nb, D_MODEL)
    hp = _prenorm(xp, norm_pre[0][None, :], tm_o)
    hs = _prenorm(xs, norm_pre[0][None, :], nb)

    outs = [[] for _ in range(8)]
    for l in range(depth):
        g_next = norm_pre[(l + 1) % depth][None, :]
        dtb_g, alog_g = _per_group(ssm_dt_bias[l], LANES), _per_group(ssm_a_log[l], LANES)
        dsk_lane = _per_lane(ssm_d[l])
        dsk_g = dsk_lane.reshape(SSM_GROUPS, 1, SSM_DIM // SSM_GROUPS)
        conv_b = ssm_conv_b[l][None, :]

        proj = _matmul(hp, w_in16, l, N_MAIN, tm_p, 1024, "inproj_prompt")
        dt_raw = _matmul(hp, w_dt_grp, l, SSM_GROUPS * LANES, tm_p, SSM_GROUPS * LANES, "dtproj_prompt")
        o_ret, r_new = _ret_prompt(proj, batch, seq, cos_p, sin_p, ret_tabs, ret_norm[l][None, :])
        o_sc, c_new = _sc_prompt(proj, batch, seq, sc_conv_w[l], sc_conv_b[l][None, :])
        ssm_pre, cx, cb_, cc, s_new = _ssd_prompt(proj, dt_raw, batch, seq, dtb_g, alog_g, dsk_g, ssm_conv_w[l], conv_b)
        xp, hp = _outproj(o_ret, o_sc, ssm_pre, xp, w_out16, l, ssm_norm[l][None, :], norm_post[l][None, :],
                          g_next, tm_o)
        outs[0].append(r_new)
        outs[1].append(c_new)
        outs[2].append(jnp.concatenate([cx, cb_, cc], axis=-1))
        outs[3].append(s_new)

        proj_s = _matmul(hs, w_in16, l, N_MAIN, nb, 1024, "inproj_decode")
        dt_s = _matmul(hs, w_dt_lane, l, SSM_DIM, nb, SSM_DIM, "dtproj_decode")
        o_ret, r_new = _ret_decode(proj_s, state_ret[l], cos_s, sin_s, gam_tab, ret_norm[l][None, :], SUBLANES)
        o_sc, c_new = _sc_decode(proj_s, state_sconv[l].reshape(nb, (SC_WIDTH - 1) * SC_DIM), sc_conv_w[l],
                                 sc_conv_b[l][None, :])
        y_s, cs_new, s_new = _ssd_decode(
            proj_s.reshape(nb, N_MAIN // LANES, LANES), dt_s.reshape(nb, xr, LANES),
            state_ssm_conv[l].reshape(nb, SSM_CONV - 1, cr, LANES), state_ssm[l],
            ssm_conv_w[l].reshape(SSM_CONV, cr, LANES), conv_b.reshape(1, cr, LANES),
            _per_lane(ssm_dt_bias[l]).reshape(1, xr, LANES), _per_lane(ssm_a_log[l]).reshape(1, xr, LANES),
            dsk_lane.reshape(1, xr, LANES), 4)
        xs, hs = _outproj(o_ret, o_sc, y_s.reshape(nb, SSM_DIM), xs, w_out16, l, ssm_norm[l][None, :],
                          norm_post[l][None, :], g_next, nb)
        outs[4].append(r_new)
        outs[5].append(c_new.reshape(nb, SC_WIDTH - 1, SC_DIM))
        outs[6].append(cs_new.reshape(nb, SSM_CONV - 1, SSM_CONV_DIM))
        outs[7].append(s_new)

    stacked = [jnp.stack(o) for o in outs]
    return (xp.reshape(batch, seq, D_MODEL), xs.reshape(nb, 1, D_MODEL), *stacked)
```

```python
import functools

import jax
import jax.numpy as jnp
import numpy as np
from jax import lax
from jax.experimental import pallas as pl
from jax.experimental.pallas import tpu as pltpu

F32 = jnp.float32
BF16 = jnp.bfloat16

D_MODEL = 2048
D_MIX = 2 * D_MODEL
RET_HEADS = 8
RET_DIM = 1024
HEAD_DIM = 128
SC_DIM = 1024
SC_WIDTH = 3
SSM_DIM = 2048
SSM_HEAD_DIM = 64
SSM_HEADS = 32
SSM_GROUPS = 4
SSM_STATE = 128
SSM_CONV = 4
SSM_CONV_DIM = SSM_DIM + 2 * SSM_GROUPS * SSM_STATE
CHUNK = 128
ROPE_BASE = 10000.0
EPS = 1e-6
PAST_LEN = 16384

OFF_Q, OFF_K, OFF_V, OFF_GR = 0, 1024, 2048, 3072
OFF_BG, OFF_CG, OFF_SH, OFF_GS = 4096, 5120, 6144, 7168
OFF_Z, OFF_X, OFF_B, OFF_C, OFF_DT = 8192, 10240, 12288, 12800, 13312
N_MAIN = OFF_DT

LANES = 128
SUBLANES = 8
V7X_VMEM_LIMIT = 56 * 1024 * 1024
PAIR = 2 * SSM_HEAD_DIM
HEADS_PER_GROUP = SSM_HEADS // SSM_GROUPS
PAIRS_PER_GROUP = HEADS_PER_GROUP // 2
N_PAIRS = SSM_HEADS // 2
ITEMS = 16


def _params(*sem):
    return pltpu.CompilerParams(dimension_semantics=sem, vmem_limit_bytes=V7X_VMEM_LIMIT)


def _silu(x):
    return x * jax.nn.sigmoid(x)


def _softplus(x):
    return jnp.maximum(x, 0.0) + jnp.log1p(jnp.exp(-jnp.abs(x)))


def _dot(a, b):
    return jnp.dot(a, b, preferred_element_type=F32)


def _bf16_parts(x, n):
    parts = []
    for _ in range(n):
        p = x.astype(BF16).astype(F32)
        parts.append(p)
        x = x - p
    return parts


def _dot_nt(a, b):
    return lax.dot_general(a, b, (((1,), (1,)), ((), ())), preferred_element_type=F32)


def _prenorm_kernel(x_ref, g_ref, o_ref):
    x = x_ref[...]
    ms = jnp.mean(x * x, axis=-1, keepdims=True)
    o_ref[...] = (x * lax.rsqrt(ms + EPS) * g_ref[...]).astype(o_ref.dtype)


def _prenorm(x, g, tm):
    m, d = x.shape
    return pl.pallas_call(
        _prenorm_kernel,
        out_shape=jax.ShapeDtypeStruct((m, d), BF16),
        grid=(m // tm,),
        in_specs=[pl.BlockSpec((tm, d), lambda i: (i, 0)), pl.BlockSpec((1, d), lambda i: (0, 0))],
        out_specs=pl.BlockSpec((tm, d), lambda i: (i, 0)),
        compiler_params=_params("parallel"),
        name="prenorm",
    )(x, g)


def _mm_kernel(x_ref, w_ref, o_ref):
    o_ref[...] = _dot(x_ref[...], w_ref[...])


def _matmul(x, w, layer, n_cols, tm, tn, name):
    m, k = x.shape
    return pl.pallas_call(
        _mm_kernel,
        out_shape=jax.ShapeDtypeStruct((m, n_cols), F32),
        grid=(m // tm, n_cols // tn),
        in_specs=[pl.BlockSpec((tm, k), lambda i, j: (i, 0)),
                  pl.BlockSpec((None, k, tn), lambda i, j: (layer, 0, j))],
        out_specs=pl.BlockSpec((tm, tn), lambda i, j: (i, j)),
        compiler_params=_params("parallel", "parallel"),
        name=name,
    )(x, w)


def _mm_cast_kernel(x_ref, w_ref, o_ref, w16_ref):
    @pl.when(pl.program_id(1) == 0)
    def _():
        w16_ref[...] = w_ref[...].astype(BF16)

    o_ref[...] = _dot(x_ref[...], w16_ref[...])


def _matmul_f32w(x, w, layer, n_cols, tm, tn, name):
    m, k = x.shape
    return pl.pallas_call(
        _mm_cast_kernel,
        out_shape=jax.ShapeDtypeStruct((m, n_cols), F32),
        grid=(n_cols // tn, m // tm),
        in_specs=[pl.BlockSpec((tm, k), lambda j, i: (i, 0)),
                  pl.BlockSpec((None, k, tn), lambda j, i: (layer, 0, j))],
        out_specs=pl.BlockSpec((tm, tn), lambda j, i: (i, j)),
        scratch_shapes=[pltpu.VMEM((k, tn), BF16)],
        compiler_params=_params("parallel", "arbitrary"),
        name=name,
    )(x, w)


def _outproj_kernel(oret_ref, osc_ref, ssm_ref, x_ref, w_ref, gssm_ref, gpost_ref, gnext_ref, y_ref, hn_ref):
    ypre = ssm_ref[...]
    ms = jnp.mean(ypre * ypre, axis=-1, keepdims=True)
    ossm = (ypre * lax.rsqrt(ms + EPS) * gssm_ref[...]).astype(BF16)
    acc = _dot(oret_ref[...], w_ref[0:RET_DIM, :])
    acc = acc + _dot(osc_ref[...], w_ref[RET_DIM:RET_DIM + SC_DIM, :])
    acc = acc + _dot(ossm, w_ref[RET_DIM + SC_DIM:, :])
    ms2 = jnp.mean(acc * acc, axis=-1, keepdims=True)
    y = x_ref[...] + acc * lax.rsqrt(ms2 + EPS) * gpost_ref[...]
    y_ref[...] = y
    ms3 = jnp.mean(y * y, axis=-1, keepdims=True)
    hn_ref[...] = (y * lax.rsqrt(ms3 + EPS) * gnext_ref[...]).astype(hn_ref.dtype)


def _outproj(oret, osc, ssm_pre, x, w_out, layer, g_ssm, g_post, g_next, tm):
    m = x.shape[0]
    row = lambda width: pl.BlockSpec((tm, width), lambda i: (i, 0))
    vec = lambda width: pl.BlockSpec((1, width), lambda i: (0, 0))
    return pl.pallas_call(
        _outproj_kernel,
        out_shape=(jax.ShapeDtypeStruct((m, D_MODEL), F32), jax.ShapeDtypeStruct((m, D_MODEL), BF16)),
        grid=(m // tm,),
        in_specs=[row(RET_DIM), row(SC_DIM), row(SSM_DIM), row(D_MODEL),
                  pl.BlockSpec((None, D_MIX, D_MODEL), lambda i: (layer, 0, 0), pipeline_mode=pl.Buffered(1)),
                  vec(SSM_DIM), vec(D_MODEL), vec(D_MODEL)],
        out_specs=(row(D_MODEL), row(D_MODEL)),
        compiler_params=_params("parallel"),
        name="outproj",
    )(oret, osc, ssm_pre, x, w_out, g_ssm, g_post, g_next)


def _rope(t, cos2, sin2):
    return t * cos2 + pltpu.roll(t, HEAD_DIM // 2, 1) * sin2


def _head_norm_gate(o, g_row, gate):
    mu = jnp.mean(o, axis=-1, keepdims=True)
    oc = o - mu
    var = jnp.mean(oc * oc, axis=-1, keepdims=True)
    return oc * lax.rsqrt(var + EPS) * g_row * _silu(gate)


def _shift_rows(tail, u, s):
    ext = jnp.concatenate([tail, u], axis=0)
    return ext[SUBLANES - s:SUBLANES - s + u.shape[0]]


def _causal_conv_rows(tail, u, w_ref, b_ref):
    width = w_ref.shape[0]
    out = b_ref[...]
    for j in range(width - 1):
        out = out + _shift_rows(tail, u, width - 1 - j) * w_ref[j:j + 1, :]
    return out + u * w_ref[width - 1:width, :]


def _ret_prompt_kernel(q_ref, k_ref, v_ref, g_ref, cos_ref, sin_ref, dmat_ref, qdec_ref, kdec_ref, cdec_ref,
                       gret_ref, o_ref, st_ref, s_scr):
    n_chunks = q_ref.shape[0] // CHUNK
    heads = q_ref.shape[1] // HEAD_DIM
    s_scr[...] = jnp.zeros_like(s_scr)

    def chunk(n, carry):
        rows = pl.ds(pl.multiple_of(n * CHUNK, CHUNK), CHUNK)
        cos2 = cos_ref[rows, :]
        sin2 = sin_ref[rows, :]
        for hh in range(heads):
            sl = slice(hh * HEAD_DIM, (hh + 1) * HEAD_DIM)
            qr = _rope(q_ref[rows, sl], cos2, sin2)
            kr = _rope(k_ref[rows, sl], cos2, sin2) * (HEAD_DIM ** -0.5)
            qb = qr.astype(BF16)
            vb = v_ref[rows, sl].astype(BF16)
            scores = _dot_nt(qb, kr.astype(BF16)) * dmat_ref[hh]
            s_old = s_scr[hh]
            o = _dot(scores.astype(BF16), vb) + _dot(qb, s_old.astype(BF16)) * qdec_ref[hh]
            kd_t = jnp.transpose(kr * kdec_ref[hh]).astype(BF16)
            s_scr[hh] = cdec_ref[hh] * s_old + _dot(kd_t, vb)
            o_ref[rows, sl] = _head_norm_gate(o, gret_ref[:, sl], g_ref[rows, sl]).astype(o_ref.dtype)
        return carry

    lax.fori_loop(0, n_chunks, chunk, 0, unroll=2)
    st_ref[0] = s_scr[...]


def _ret_prompt(proj, batch, seq, cos2, sin2, tabs, g_ret):
    hp = 4
    width = hp * HEAD_DIM
    col = lambda off: pl.BlockSpec((seq, width), lambda b, h, off=off: (b, off // width + h))
    tab = pl.BlockSpec((hp, CHUNK, CHUNK), lambda b, h: (h, 0, 0))
    full = pl.BlockSpec((seq, HEAD_DIM), lambda b, h: (0, 0))
    return pl.pallas_call(
        _ret_prompt_kernel,
        out_shape=(jax.ShapeDtypeStruct((batch * seq, RET_DIM), BF16),
                   jax.ShapeDtypeStruct((batch, RET_HEADS, HEAD_DIM, HEAD_DIM), F32)),
        grid=(batch, RET_HEADS // hp),
        in_specs=[col(OFF_Q), col(OFF_K), col(OFF_V), col(OFF_GR), full, full, tab, tab, tab, tab,
                  pl.BlockSpec((1, width), lambda b, h: (0, h))],
        out_specs=(pl.BlockSpec((seq, width), lambda b, h: (b, h)),
                   pl.BlockSpec((1, hp, HEAD_DIM, HEAD_DIM), lambda b, h: (b, h, 0, 0))),
        scratch_shapes=[pltpu.VMEM((hp, HEAD_DIM, HEAD_DIM), F32)],
        compiler_params=_params("parallel", "parallel"),
        name="ret_prompt",
    )(proj, proj, proj, proj, cos2, sin2, *tabs, g_ret)


def _ret_decode_kernel(q_ref, k_ref, v_ref, g_ref, cos_ref, sin_ref, gam_ref, gret_ref, s_ref, o_ref, sn_ref):
    bt = q_ref.shape[0]
    assert 2 * bt == ITEMS
    cos2 = cos_ref[...]
    sin2 = sin_ref[...]
    lane = lax.broadcasted_iota(jnp.int32, (LANES, LANES), 1)
    zero = jnp.zeros((bt, HEAD_DIM), F32)
    one = jnp.ones((bt, HEAD_DIM), F32)
    for hp in range(RET_HEADS // 2):
        qr, kr, v, gam, sls = [], [], [], [], []
        for hh in range(2):
            h = 2 * hp + hh
            sl = slice(h * HEAD_DIM, (h + 1) * HEAD_DIM)
            sls.append(sl)
            gam.append(gam_ref[h][0:1, :])
            qr.append(_rope(q_ref[:, sl], cos2, sin2))
            kr.append(_rope(k_ref[:, sl], cos2, sin2) * (HEAD_DIM ** -0.5))
            v.append(v_ref[:, sl])
        k2 = [_bf16_parts(t, 2) for t in kr]
        q3 = [_bf16_parts(t, 3) for t in qr]
        v2 = [_bf16_parts(t, 2) for t in v]
        both = lambda parts, n: [parts[0][n], parts[1][n]]
        a = jnp.concatenate(both(k2, 0) + both(k2, 0) + both(k2, 1) + both(q3, 0) + both(q3, 1) + both(q3, 2)
                            + [zero] * 4, axis=0)
        cols = jnp.transpose(a)
        r_kv = jnp.concatenate(both(v2, 0) + both(v2, 1) + both(v2, 0) + [zero] * 10, axis=0)
        r_q = jnp.concatenate([zero] * 6 + [one] * 6 + [zero] * 4, axis=0)
        r = jnp.concatenate([r_kv, r_q], axis=1).astype(BF16)
        q_s = [[], []]
        for i in range(ITEMS):
            hh, j = divmod(i, bt)
            h = 2 * hp + hh
            out = _dot(jnp.where((lane & (ITEMS - 1)) == i, cols, 0.0).astype(BF16), r)
            s_old = s_ref[j, h]
            sn_ref[j, h] = gam[hh] * s_old + out[:, :HEAD_DIM]
            q_s[hh].append(jnp.sum(s_old * out[:, HEAD_DIM:], axis=0, keepdims=True))
        for hh in range(2):
            qk = jnp.sum(qr[hh] * kr[hh], axis=-1, keepdims=True)
            o = qk * v[hh] + jnp.concatenate(q_s[hh], axis=0) * gam[hh]
            o_ref[:, sls[hh]] = _head_norm_gate(o, gret_ref[:, sls[hh]], g_ref[:, sls[hh]]).astype(o_ref.dtype)


def _ret_decode(proj, states, layer, cos2, sin2, gam_tab, g_ret, bt):
    nb = proj.shape[0]
    col = lambda off: pl.BlockSpec((bt, RET_DIM), lambda i, off=off: (i, off // RET_DIM))
    st = pl.BlockSpec((bt, RET_HEADS, HEAD_DIM, HEAD_DIM), lambda i: (i, 0, 0, 0))
    st_in = pl.BlockSpec((None, bt, RET_HEADS, HEAD_DIM, HEAD_DIM), lambda i: (layer, i, 0, 0, 0))
    return pl.pallas_call(
        _ret_decode_kernel,
        out_shape=(jax.ShapeDtypeStruct((nb, RET_DIM), BF16), jax.ShapeDtypeStruct(states.shape[1:], F32)),
        grid=(nb // bt,),
        in_specs=[col(OFF_Q), col(OFF_K), col(OFF_V), col(OFF_GR),
                  pl.BlockSpec((1, HEAD_DIM), lambda i: (0, 0)), pl.BlockSpec((1, HEAD_DIM), lambda i: (0, 0)),
                  pl.BlockSpec((RET_HEADS, SUBLANES, LANES), lambda i: (0, 0, 0)),
                  pl.BlockSpec((1, RET_DIM), lambda i: (0, 0)), st_in],
        out_specs=(pl.BlockSpec((bt, RET_DIM), lambda i: (i, 0)), st),
        compiler_params=_params("parallel"),
        name="ret_decode",
    )(proj, proj, proj, proj, cos2, sin2, gam_tab, g_ret, states)


def _sc_prompt_kernel(bg_ref, cg_ref, sh_ref, gs_ref, w_ref, b_ref, o_ref, st_ref):
    seq, width = bg_ref.shape

    def chunk(n, tail):
        rows = pl.ds(pl.multiple_of(n * CHUNK, CHUNK), CHUNK)
        u = cg_ref[rows, :] * sh_ref[rows, :]
        conv = _causal_conv_rows(tail, u, w_ref, b_ref)
        o_ref[rows, :] = (bg_ref[rows, :] * conv * _silu(gs_ref[rows, :])).astype(o_ref.dtype)
        return u[CHUNK - SUBLANES:, :]

    tail = lax.fori_loop(0, seq // CHUNK, chunk, jnp.zeros((SUBLANES, width), F32))
    st_ref[0] = tail[SUBLANES - (SC_WIDTH - 1):, :]


def _sc_prompt(proj, batch, seq, w, b):
    width = 256
    col = lambda off: pl.BlockSpec((seq, width), lambda bi, c, off=off: (bi, off // width + c))
    return pl.pallas_call(
        _sc_prompt_kernel,
        out_shape=(jax.ShapeDtypeStruct((batch * seq, SC_DIM), BF16),
                   jax.ShapeDtypeStruct((batch, SC_WIDTH - 1, SC_DIM), F32)),
        grid=(batch, SC_DIM // width),
        in_specs=[col(OFF_BG), col(OFF_CG), col(OFF_SH), col(OFF_GS),
                  pl.BlockSpec((SC_WIDTH, width), lambda bi, c: (0, c)),
                  pl.BlockSpec((1, width), lambda bi, c: (0, c))],
        out_specs=(pl.BlockSpec((seq, width), lambda bi, c: (bi, c)),
                   pl.BlockSpec((1, SC_WIDTH - 1, width), lambda bi, c: (bi, 0, c))),
        compiler_params=_params("parallel", "parallel"),
        name="sc_prompt",
    )(proj, proj, proj, proj, w, b)


def _sc_decode_kernel(bg_ref, cg_ref, sh_ref, gs_ref, st_ref, w_ref, b_ref, o_ref, stn_ref):
    u = cg_ref[...] * sh_ref[...]
    r0 = st_ref[:, 0:SC_DIM]
    r1 = st_ref[:, SC_DIM:]
    conv = b_ref[...] + r0 * w_ref[0:1, :] + r1 * w_ref[1:2, :] + u * w_ref[2:3, :]
    o_ref[...] = (bg_ref[...] * conv * _silu(gs_ref[...])).astype(o_ref.dtype)
    stn_ref[:, 0:SC_DIM] = r1
    stn_ref[:, SC_DIM:] = u


def _sc_decode(proj, state2d, w, b):
    nb = proj.shape[0]
    col = lambda off: pl.BlockSpec((nb, SC_DIM), lambda i, off=off: (0, off // SC_DIM))
    whole = lambda shape: pl.BlockSpec(shape, lambda i: (0,) * len(shape))
    return pl.pallas_call(
        _sc_decode_kernel,
        out_shape=(jax.ShapeDtypeStruct((nb, SC_DIM), BF16), jax.ShapeDtypeStruct(state2d.shape, F32)),
        grid=(1,),
        in_specs=[col(OFF_BG), col(OFF_CG), col(OFF_SH), col(OFF_GS), whole(state2d.shape),
                  whole((SC_WIDTH, SC_DIM)), whole((1, SC_DIM))],
        out_specs=(whole((nb, SC_DIM)), whole(state2d.shape)),
        compiler_params=_params("arbitrary"),
        name="sc_decode",
    )(proj, proj, proj, proj, state2d, w, b)


def _chunk_cumsum(a, row):
    s = 1
    while s < CHUNK:
        a = a + jnp.where(row >= s, pltpu.roll(a, s, 0), 0.0)
        s *= 2
    return a


def _ssd_prompt_kernel(z_ref, x_ref, b_ref, c_ref, dtr_ref, dtb_ref, alog_ref, dsk_ref,
                       cwx_ref, cwb_ref, cwc_ref, cbx_ref, cbb_ref, cbc_ref,
                       y_ref, stx_ref, stb_ref, stc_ref, sst_ref, s_scr):
    seq = z_ref.shape[0]
    row = lax.broadcasted_iota(jnp.int32, (CHUNK, CHUNK), 0)
    lane = lax.broadcasted_iota(jnp.int32, (CHUNK, CHUNK), 1)
    causal = row >= lane
    low_lanes = lane < SSM_HEAD_DIM
    low_rows = row < SSM_HEAD_DIM
    a_neg = -jnp.exp(alog_ref[0])
    dt_bias = dtb_ref[0]
    s_scr[...] = jnp.zeros_like(s_scr)

    def chunk(n, tails):
        tx, tb, tc = tails
        rows = pl.ds(pl.multiple_of(n * CHUNK, CHUNK), CHUNK)
        x_raw = x_ref[rows, :]
        b_raw = b_ref[rows, :]
        c_raw = c_ref[rows, :]
        xc = _silu(_causal_conv_rows(tx, x_raw, cwx_ref, cbx_ref))
        bb = _silu(_causal_conv_rows(tb, b_raw, cwb_ref, cbb_ref)).astype(BF16)
        cb16 = _silu(_causal_conv_rows(tc, c_raw, cwc_ref, cbc_ref)).astype(BF16)
        dt = _softplus(dtr_ref[rows, :] + dt_bias)
        acum = _chunk_cumsum(dt * a_neg, row)
        acum_t = jnp.transpose(acum)
        dt_t = jnp.transpose(dt)
        cb = _dot_nt(cb16, bb)
        for pp in range(PAIRS_PER_GROUP):
            sl = slice(pp * PAIR, (pp + 1) * PAIR)
            x_pair = xc[:, sl]
            m, e_col, w_row, dec = [], [], [], []
            for hh in range(2):
                c = 2 * pp + hh
                a_col = jnp.broadcast_to(acum[:, c:c + 1], (CHUNK, CHUNK))
                a_row = acum_t[c:c + 1, :]
                dt_row = dt_t[c:c + 1, :]
                lmat = jnp.exp(jnp.where(causal, a_col - a_row, -jnp.inf))
                m.append((cb * lmat * dt_row).astype(BF16))
                e_col.append(jnp.exp(a_col))
                last = a_col[CHUNK - 1:CHUNK, :]
                w_row.append(jnp.exp(last - a_row) * dt_row)
                dec.append(jnp.exp(last))
            x_lo = jnp.where(low_lanes, x_pair, 0.0).astype(BF16)
            x_hi = jnp.where(low_lanes, 0.0, x_pair).astype(BF16)
            s_old = s_scr[pp]
            y = _dot(m[0], x_lo) + _dot(m[1], x_hi)
            y = y + _dot_nt(cb16, s_old.astype(BF16)) * jnp.where(low_lanes, e_col[0], e_col[1])
            y = y + dsk_ref[0][:, sl] * x_pair
            y_ref[rows, sl] = y * _silu(z_ref[rows, sl])
            xw = (jnp.transpose(x_pair) * jnp.where(low_rows, w_row[0], w_row[1])).astype(BF16)
            s_scr[pp] = jnp.where(low_rows, dec[0], dec[1]) * s_old + _dot(xw, bb)
        cut = CHUNK - SUBLANES
        return x_raw[cut:, :], b_raw[cut:, :], c_raw[cut:, :]

    zeros = lambda r: jnp.zeros((SUBLANES, r.shape[1]), F32)
    tx, tb, tc = lax.fori_loop(0, seq // CHUNK, chunk, (zeros(x_ref), zeros(b_ref), zeros(c_ref)))
    keep = SUBLANES - (SSM_CONV - 1)
    stx_ref[0] = tx[keep:, :]
    stb_ref[0] = tb[keep:, :]
    stc_ref[0] = tc[keep:, :]
    for pp in range(PAIRS_PER_GROUP):
        for hh in range(2):
            sst_ref[0, 2 * pp + hh] = s_scr[pp, hh * SSM_HEAD_DIM:(hh + 1) * SSM_HEAD_DIM, :]


def _ssd_prompt(proj, dt_raw, batch, seq, dtb, alog, dsk, conv_w, conv_b):
    gw = SSM_DIM // SSM_GROUPS
    n = SSM_STATE
    bcol = lambda off, width: pl.BlockSpec((seq, width), lambda b, g, off=off, width=width: (b, off // width + g))
    grp = lambda width: pl.BlockSpec((1, 1, width), lambda b, g: (g, 0, 0))
    cw = lambda off, width: pl.BlockSpec((SSM_CONV, width), lambda b, g, off=off, width=width: (0, off // width + g))
    cbias = lambda off, width: pl.BlockSpec((1, width), lambda b, g, off=off, width=width: (0, off // width + g))
    st = lambda width: pl.BlockSpec((1, SSM_CONV - 1, width), lambda b, g: (b, 0, g))
    return pl.pallas_call(
        _ssd_prompt_kernel,
        out_shape=(jax.ShapeDtypeStruct((batch * seq, SSM_DIM), F32),
                   jax.ShapeDtypeStruct((batch, SSM_CONV - 1, SSM_DIM), F32),
                   jax.ShapeDtypeStruct((batch, SSM_CONV - 1, SSM_GROUPS * n), F32),
                   jax.ShapeDtypeStruct((batch, SSM_CONV - 1, SSM_GROUPS * n), F32),
                   jax.ShapeDtypeStruct((batch, SSM_HEADS, SSM_HEAD_DIM, n), F32)),
        grid=(batch, SSM_GROUPS),
        in_specs=[bcol(OFF_Z, gw), bcol(OFF_X, gw), bcol(OFF_B, n), bcol(OFF_C, n),
                  pl.BlockSpec((seq, LANES), lambda b, g: (b, g)),
                  grp(LANES), grp(LANES), grp(gw),
                  cw(0, gw), cw(SSM_DIM, n), cw(SSM_DIM + SSM_GROUPS * n, n),
                  cbias(0, gw), cbias(SSM_DIM, n), cbias(SSM_DIM + SSM_GROUPS * n, n)],
        out_specs=(pl.BlockSpec((seq, gw), lambda b, g: (b, g)), st(gw), st(n), st(n),
                   pl.BlockSpec((1, HEADS_PER_GROUP, SSM_HEAD_DIM, n), lambda b, g: (b, g, 0, 0))),
        scratch_shapes=[pltpu.VMEM((PAIRS_PER_GROUP, PAIR, n), F32)],
        compiler_params=_params("parallel", "parallel"),
        name="ssd_prompt",
    )(proj, proj, proj, proj, dt_raw, dtb, alog, dsk, conv_w, conv_w, conv_w, conv_b, conv_b, conv_b)


def _ssd_decode_kernel(z_ref, x_ref, bc_ref, dtr_ref, cst_ref, s_ref, cw_ref, cb_ref, dtb_ref, alog_ref, dsk_ref,
                       y_ref, cstn_ref, sn_ref):
    bt = z_ref.shape[0]
    xr = SSM_DIM // LANES
    gr = SSM_GROUPS * SSM_STATE // LANES
    u = jnp.concatenate([x_ref[...], bc_ref[...]], axis=1)
    conv = cb_ref[...]
    for j in range(SSM_CONV - 1):
        conv = conv + cst_ref[:, j] * cw_ref[j]
        cstn_ref[:, j] = u if j == SSM_CONV - 2 else cst_ref[:, j + 1]
    act = _silu(conv + u * cw_ref[SSM_CONV - 1])
    xs = act[:, 0:xr]
    bm = act[:, xr:xr + gr]
    cm = act[:, xr + gr:]
    dt = _softplus(dtr_ref[...] + dtb_ref[...])
    ea = jnp.exp(dt * -jnp.exp(alog_ref[...]))
    xdt = xs * dt
    assert xr == ITEMS == N_PAIRS
    lane = lax.broadcasted_iota(jnp.int32, (LANES, LANES), 1)
    row16 = lax.broadcasted_iota(jnp.int32, (xr, LANES), 0)
    zeros = lambda n: jnp.zeros((n, LANES), F32)
    r_ea = jnp.concatenate([zeros(3 * ITEMS), jnp.ones((3 * ITEMS, LANES), F32), zeros(2 * ITEMS)], axis=0)
    for j in range(bt):
        cbt = zeros(xr)
        b_rows = zeros(xr)
        for g in range(SSM_GROUPS):
            in_group = row16 // PAIRS_PER_GROUP == g
            cb_g = jnp.sum(cm[j, g:g + 1, :] * bm[j, g:g + 1, :], axis=-1, keepdims=True)
            cbt = jnp.where(in_group, cb_g, cbt)
            b_rows = jnp.where(in_group, bm[j, g:g + 1, :], b_rows)
        xd2 = _bf16_parts(xdt[j], 2)
        ea3 = _bf16_parts(ea[j], 3)
        b2 = _bf16_parts(b_rows, 2)
        a = jnp.concatenate([xd2[0], xd2[0], xd2[1]] + ea3 + [zeros(2 * ITEMS)], axis=0)
        cols = jnp.transpose(a)
        r_xb = jnp.concatenate([b2[0], b2[1], b2[0], zeros(5 * ITEMS)], axis=0)
        r = jnp.concatenate([r_xb, r_ea], axis=1).astype(BF16)
        y_t = jnp.zeros((LANES, LANES), F32)
        for pp in range(N_PAIRS):
            g = pp // PAIRS_PER_GROUP
            out = _dot(jnp.where((lane & (ITEMS - 1)) == pp, cols, 0.0).astype(BF16), r)
            s_old = s_ref[j, 2 * pp:2 * pp + 2].reshape(PAIR, SSM_STATE)
            y_col = jnp.sum(s_old * cm[j, g:g + 1, :], axis=-1, keepdims=True)
            y_t = jnp.where(lane == pp, y_col, y_t)
            s_new = out[:, SSM_STATE:] * s_old + out[:, :SSM_STATE]
            sn_ref[j, 2 * pp:2 * pp + 2] = s_new.reshape(2, SSM_HEAD_DIM, SSM_STATE)
        y_inter = jnp.transpose(y_t)[0:xr, :]
        y = cbt * xdt[j] + y_inter * ea[j] + dsk_ref[0] * xs[j]
        y_ref[j] = y * _silu(z_ref[j])


def _ssd_decode(proj3, dt_raw3, cstates, states, layer, conv_w3, conv_b3, dtb3, alog3, dsk3, bt):
    nb = proj3.shape[0]
    xr = SSM_DIM // LANES
    cr = SSM_CONV_DIM // LANES
    whole = lambda shape: pl.BlockSpec(shape, lambda i: (0,) * len(shape))
    st = pl.BlockSpec((bt, SSM_HEADS, SSM_HEAD_DIM, SSM_STATE), lambda i: (i, 0, 0, 0))
    cst = pl.BlockSpec((bt, SSM_CONV - 1, cr, LANES), lambda i: (i, 0, 0, 0))
    st_in = pl.BlockSpec((None, bt, SSM_HEADS, SSM_HEAD_DIM, SSM_STATE), lambda i: (layer, i, 0, 0, 0))
    cst_in = pl.BlockSpec((None, bt, SSM_CONV - 1, cr, LANES), lambda i: (layer, i, 0, 0, 0))
    return pl.pallas_call(
        _ssd_decode_kernel,
        out_shape=(jax.ShapeDtypeStruct((nb, xr, LANES), F32), jax.ShapeDtypeStruct(cstates.shape[1:], F32),
                   jax.ShapeDtypeStruct(states.shape[1:], F32)),
        grid=(nb // bt,),
        in_specs=[pl.BlockSpec((bt, xr, LANES), lambda i: (i, OFF_Z // SSM_DIM, 0)),
                  pl.BlockSpec((bt, xr, LANES), lambda i: (i, OFF_X // SSM_DIM, 0)),
                  pl.BlockSpec((bt, SUBLANES, LANES), lambda i: (i, OFF_B // (SUBLANES * LANES), 0)),
                  pl.BlockSpec((bt, xr, LANES), lambda i: (i, 0, 0)),
                  cst_in, st_in, whole((SSM_CONV, cr, LANES)), whole((1, cr, LANES)),
                  whole((1, xr, LANES)), whole((1, xr, LANES)), whole((1, xr, LANES))],
        out_specs=(pl.BlockSpec((bt, xr, LANES), lambda i: (i, 0, 0)), cst, st),
        compiler_params=_params("parallel"),
        name="ssd_decode",
    )(proj3, proj3, proj3, dt_raw3, cstates, states, conv_w3, conv_b3, dtb3, alog3, dsk3)


def _rope_tables(pos):
    half = HEAD_DIM // 2
    inv = ROPE_BASE ** (-jnp.arange(half, dtype=F32) / half)
    ang = pos.astype(F32)[:, None] * inv[None, :]
    cos, sin = jnp.cos(ang), jnp.sin(ang)
    return jnp.concatenate([cos, cos], axis=-1), jnp.concatenate([-sin, sin], axis=-1)


def _retention_tables():
    lg = jnp.log(1.0 - 2.0 ** (-5.0 - jnp.arange(RET_HEADS, dtype=F32)))
    i = jnp.arange(CHUNK, dtype=F32)
    diff = i[:, None] - i[None, :]
    dmat = jnp.exp(jnp.where((diff >= 0)[None], diff[None] * lg[:, None, None], -jnp.inf))
    full = lambda t: jnp.broadcast_to(t, (RET_HEADS, CHUNK, CHUNK))
    q_dec = full(jnp.exp((i + 1.0)[None, :, None] * lg[:, None, None]))
    k_dec = full(jnp.exp((CHUNK - 1.0 - i)[None, :, None] * lg[:, None, None]))
    c_dec = full(jnp.exp(CHUNK * lg)[:, None, None])
    gam = jnp.broadcast_to(jnp.exp(lg)[:, None, None], (RET_HEADS, SUBLANES, LANES))
    return (dmat, q_dec, k_dec, c_dec), gam


def _per_group(v, width):
    v = v.astype(F32).reshape(SSM_GROUPS, 1, HEADS_PER_GROUP)
    return jnp.pad(v, ((0, 0), (0, 0), (0, width - HEADS_PER_GROUP)))


def _per_lane(v):
    return jnp.repeat(v.astype(F32), SSM_HEAD_DIM)[None, :]


def kernel(x_prompt, x_sample, state_ret, state_sconv, state_ssm_conv, state_ssm, w_in, w_out, norm_pre, norm_post,
           ret_norm, sc_conv_w, sc_conv_b, ssm_conv_w, ssm_conv_b, ssm_dt_bias, ssm_a_log, ssm_d, ssm_norm):
    batch, seq, _ = x_prompt.shape
    nb = x_sample.shape[0]
    depth = w_in.shape[0]
    mp = batch * seq
    xr = SSM_DIM // LANES
    cr = SSM_CONV_DIM // LANES

    cos_p, sin_p = _rope_tables(jnp.arange(seq))
    cos_s, sin_s = _rope_tables(PAST_LEN + jnp.arange(1))
    ret_tabs, gam_tab = _retention_tables()

    w_out16 = w_out.astype(BF16)
    w_dt = w_in[:, :, OFF_DT:]
    w_dt_grp = jnp.pad(w_dt.reshape(depth, D_MODEL, SSM_GROUPS, HEADS_PER_GROUP),
                       ((0, 0), (0, 0), (0, 0), (0, LANES - HEADS_PER_GROUP)))
    w_dt_grp = w_dt_grp.reshape(depth, D_MODEL, SSM_GROUPS * LANES).astype(BF16)
    w_dt_lane = jnp.repeat(w_dt, SSM_HEAD_DIM, axis=2).astype(BF16)

    tm_p = min(1024, mp)
    tm_o = min(256, mp)
    xp = x_prompt.reshape(mp, D_MODEL)
    xs = x_sample.reshape(nb, D_MODEL)
    hp = _prenorm(xp, norm_pre[0][None, :], tm_o)
    hs = _prenorm(xs, norm_pre[0][None, :], nb)

    outs = [[] for _ in range(8)]
    for l in range(depth):
        g_next = norm_pre[(l + 1) % depth][None, :]
        dtb_g, alog_g = _per_group(ssm_dt_bias[l], LANES), _per_group(ssm_a_log[l], LANES)
        dsk_lane = _per_lane(ssm_d[l])
        dsk_g = dsk_lane.reshape(SSM_GROUPS, 1, SSM_DIM // SSM_GROUPS)
        conv_b = ssm_conv_b[l][None, :]

        proj = _matmul_f32w(hp, w_in, l, N_MAIN, tm_p, 1024, "inproj_prompt")
        dt_raw = _matmul(hp, w_dt_grp, l, SSM_GROUPS * LANES, tm_p, SSM_GROUPS * LANES, "dtproj_prompt")
        o_ret, r_new = _ret_prompt(proj, batch, seq, cos_p, sin_p, ret_tabs, ret_norm[l][None, :])
        o_sc, c_new = _sc_prompt(proj, batch, seq, sc_conv_w[l], sc_conv_b[l][None, :])
        ssm_pre, cx, cb_, cc, s_new = _ssd_prompt(proj, dt_raw, batch, seq, dtb_g, alog_g, dsk_g, ssm_conv_w[l], conv_b)
        xp, hp = _outproj(o_ret, o_sc, ssm_pre, xp, w_out16, l, ssm_norm[l][None, :], norm_post[l][None, :],
                          g_next, tm_o)
        outs[0].append(r_new)
        outs[1].append(c_new)
        outs[2].append(jnp.concatenate([cx, cb_, cc], axis=-1))
        outs[3].append(s_new)

        proj_s = _matmul_f32w(hs, w_in, l, N_MAIN, nb, 1024, "inproj_decode")
        dt_s = _matmul(hs, w_dt_lane, l, SSM_DIM, nb, SSM_DIM, "dtproj_decode")
        o_ret, r_new = _ret_decode(proj_s, state_ret, l, cos_s, sin_s, gam_tab, ret_norm[l][None, :], SUBLANES)
        o_sc, c_new = _sc_decode(proj_s, state_sconv[l].reshape(nb, (SC_WIDTH - 1) * SC_DIM), sc_conv_w[l],
                                 sc_conv_b[l][None, :])
        y_s, cs_new, s_new = _ssd_decode(
            proj_s.reshape(nb, N_MAIN // LANES, LANES), dt_s.reshape(nb, xr, LANES),
            state_ssm_conv.reshape(depth, nb, SSM_CONV - 1, cr, LANES), state_ssm, l,
            ssm_conv_w[l].reshape(SSM_CONV, cr, LANES), conv_b.reshape(1, cr, LANES),
            _per_lane(ssm_dt_bias[l]).reshape(1, xr, LANES), _per_lane(ssm_a_log[l]).reshape(1, xr, LANES),
            dsk_lane.reshape(1, xr, LANES), 4)
        xs, hs = _outproj(o_ret, o_sc, y_s.reshape(nb, SSM_DIM), xs, w_out16, l, ssm_norm[l][None, :],
                          norm_post[l][None, :], g_next, nb)
        outs[4].append(r_new)
        outs[5].append(c_new.reshape(nb, SC_WIDTH - 1, SC_DIM))
        outs[6].append(cs_new.reshape(nb, SSM_CONV - 1, SSM_CONV_DIM))
        outs[7].append(s_new)

    stacked = [jnp.stack(o) for o in outs]
    return (xp.reshape(batch, seq, D_MODEL), xs.reshape(nb, 1, D_MODEL), *stacked)
```

```python
import functools

import jax
import jax.numpy as jnp
import numpy as np
from jax import lax
from jax.experimental import pallas as pl
from jax.experimental.pallas import tpu as pltpu

F32 = jnp.float32
BF16 = jnp.bfloat16

D_MODEL = 2048
D_MIX = 2 * D_MODEL
RET_HEADS = 8
RET_DIM = 1024
HEAD_DIM = 128
SC_DIM = 1024
SC_WIDTH = 3
SSM_DIM = 2048
SSM_HEAD_DIM = 64
SSM_HEADS = 32
SSM_GROUPS = 4
SSM_STATE = 128
SSM_CONV = 4
SSM_CONV_DIM = SSM_DIM + 2 * SSM_GROUPS * SSM_STATE
CHUNK = 128
PROJ_ROWS = 2 * CHUNK
ROPE_BASE = 10000.0
EPS = 1e-6
PAST_LEN = 16384

OFF_Q, OFF_K, OFF_V, OFF_GR = 0, 1024, 2048, 3072
OFF_BG, OFF_CG, OFF_SH, OFF_GS = 4096, 5120, 6144, 7168
OFF_Z, OFF_X, OFF_B, OFF_C, OFF_DT = 8192, 10240, 12288, 12800, 13312
N_MAIN = OFF_DT

LANES = 128
SUBLANES = 8
V7X_VMEM_LIMIT = 56 * 1024 * 1024
PAIR = 2 * SSM_HEAD_DIM
HEADS_PER_GROUP = SSM_HEADS // SSM_GROUPS
PAIRS_PER_GROUP = HEADS_PER_GROUP // 2
N_PAIRS = SSM_HEADS // 2
ITEMS = 16


def _params(*sem):
    return pltpu.CompilerParams(dimension_semantics=sem, vmem_limit_bytes=V7X_VMEM_LIMIT)


def _silu(x):
    return x * jax.nn.sigmoid(x)


def _softplus(x):
    return jnp.maximum(x, 0.0) + jnp.log1p(jnp.exp(-jnp.abs(x)))


def _dot(a, b):
    return jnp.dot(a, b, preferred_element_type=F32)


def _bf16_parts(x, n):
    parts = []
    for _ in range(n):
        p = x.astype(BF16).astype(F32)
        parts.append(p)
        x = x - p
    return parts


def _dot_nt(a, b):
    return lax.dot_general(a, b, (((1,), (1,)), ((), ())), preferred_element_type=F32)


def _prenorm_kernel(x_ref, g_ref, o_ref):
    x = x_ref[...]
    ms = jnp.mean(x * x, axis=-1, keepdims=True)
    o_ref[...] = (x * lax.rsqrt(ms + EPS) * g_ref[...]).astype(o_ref.dtype)


def _prenorm(x, g, tm):
    m, d = x.shape
    return pl.pallas_call(
        _prenorm_kernel,
        out_shape=jax.ShapeDtypeStruct((m, d), BF16),
        grid=(m // tm,),
        in_specs=[pl.BlockSpec((tm, d), lambda i: (i, 0)), pl.BlockSpec((1, d), lambda i: (0, 0))],
        out_specs=pl.BlockSpec((tm, d), lambda i: (i, 0)),
        compiler_params=_params("parallel"),
        name="prenorm",
    )(x, g)


def _mm_nt_kernel(x_ref, w_ref, o_ref):
    o_ref[...] = _dot_nt(x_ref[...], w_ref[...].astype(BF16))


def _matmul_nt(x, w_t, layer, n_cols, tn, name):
    m, k = x.shape
    return pl.pallas_call(
        _mm_nt_kernel,
        out_shape=jax.ShapeDtypeStruct((m, n_cols), F32),
        grid=(n_cols // tn,),
        in_specs=[pl.BlockSpec((m, k), lambda j: (0, 0)),
                  pl.BlockSpec((None, tn, k), lambda j: (layer, j, 0))],
        out_specs=pl.BlockSpec((m, tn), lambda j: (0, j)),
        compiler_params=_params("parallel"),
        name=name,
    )(x, w_t)


def _outproj_kernel(oret_ref, osc_ref, ssm_ref, x_ref, w_ref, gssm_ref, gpost_ref, gnext_ref, y_ref, hn_ref):
    ypre = ssm_ref[...]
    ms = jnp.mean(ypre * ypre, axis=-1, keepdims=True)
    ossm = (ypre * lax.rsqrt(ms + EPS) * gssm_ref[...]).astype(BF16)
    acc = _dot(oret_ref[...], w_ref[0:RET_DIM, :])
    acc = acc + _dot(osc_ref[...], w_ref[RET_DIM:RET_DIM + SC_DIM, :])
    acc = acc + _dot(ossm, w_ref[RET_DIM + SC_DIM:, :])
    ms2 = jnp.mean(acc * acc, axis=-1, keepdims=True)
    y = x_ref[...] + acc * lax.rsqrt(ms2 + EPS) * gpost_ref[...]
    y_ref[...] = y
    ms3 = jnp.mean(y * y, axis=-1, keepdims=True)
    hn_ref[...] = (y * lax.rsqrt(ms3 + EPS) * gnext_ref[...]).astype(hn_ref.dtype)


def _outproj(oret, osc, ssm_pre, x, w_out, layer, g_ssm, g_post, g_next, tm):
    m = x.shape[0]
    row = lambda width: pl.BlockSpec((tm, width), lambda i: (i, 0))
    vec = lambda width: pl.BlockSpec((1, width), lambda i: (0, 0))
    return pl.pallas_call(
        _outproj_kernel,
        out_shape=(jax.ShapeDtypeStruct((m, D_MODEL), F32), jax.ShapeDtypeStruct((m, D_MODEL), BF16)),
        grid=(m // tm,),
        in_specs=[row(RET_DIM), row(SC_DIM), row(SSM_DIM), row(D_MODEL),
                  pl.BlockSpec((None, D_MIX, D_MODEL), lambda i: (layer, 0, 0), pipeline_mode=pl.Buffered(1)),
                  vec(SSM_DIM), vec(D_MODEL), vec(D_MODEL)],
        out_specs=(row(D_MODEL), row(D_MODEL)),
        compiler_params=_params("parallel"),
        name="outproj",
    )(oret, osc, ssm_pre, x, w_out, g_ssm, g_post, g_next)


def _rope(t, cos2, sin2):
    return t * cos2 + pltpu.roll(t, HEAD_DIM // 2, 1) * sin2


def _head_norm_gate(o, g_row, gate):
    mu = jnp.mean(o, axis=-1, keepdims=True)
    oc = o - mu
    var = jnp.mean(oc * oc, axis=-1, keepdims=True)
    return oc * lax.rsqrt(var + EPS) * g_row * _silu(gate)


def _shift_rows(tail, u, s):
    ext = jnp.concatenate([tail, u], axis=0)
    return ext[SUBLANES - s:SUBLANES - s + u.shape[0]]


def _causal_conv_rows(tail, u, w_ref, b_ref):
    width = w_ref.shape[0]
    out = b_ref[...]
    for j in range(width - 1):
        out = out + _shift_rows(tail, u, width - 1 - j) * w_ref[j:j + 1, :]
    return out + u * w_ref[width - 1:width, :]


def _load_weight_cols(w_refs, w16_ref):
    off = 0
    for w_ref in w_refs:
        n = w_ref.shape[0]
        for r in range(0, n, LANES):
            blk = w_ref[r:min(r + LANES, n), :]
            if blk.shape[0] < LANES:
                blk = jnp.concatenate([blk, jnp.zeros((LANES - blk.shape[0], blk.shape[1]), F32)], axis=0)
            w16_ref[:, off:off + LANES] = jnp.transpose(blk).astype(BF16)
            off += LANES


def _projected_chunks(h_ref, w16_ref, p_even, p_odd, chunk_fn, carry):
    steps = h_ref.shape[0] // PROJ_ROWS
    per_step = PROJ_ROWS // CHUNK

    def project(step, p_ref):
        rows = pl.ds(pl.multiple_of(step * PROJ_ROWS, PROJ_ROWS), PROJ_ROWS)
        p_ref[...] = _dot(h_ref[rows, :], w16_ref[...])

    def mix(step, p_ref, carry):
        for c in range(per_step):
            carry = chunk_fn(p_ref, c * CHUNK, step * per_step + c, carry)
        return carry

    def body(i, carry):
        project(2 * i + 1, p_odd)
        carry = mix(2 * i, p_even, carry)
        project(2 * i + 2, p_even)
        return mix(2 * i + 1, p_odd, carry)

    project(0, p_even)
    full = (steps - 1) // 2
    carry = lax.fori_loop(0, full, body, carry)
    if steps - 2 * full == 2:
        project(2 * full + 1, p_odd)
        carry = mix(2 * full, p_even, carry)
        return mix(2 * full + 1, p_odd, carry)
    return mix(2 * full, p_even, carry)


def _ret_prompt_kernel(h_ref, wq_ref, wk_ref, wv_ref, wg_ref, cos_ref, sin_ref, dmat_ref, qdec_ref, kdec_ref,
                       cdec_ref, gret_ref, o_ref, st_ref, w16_ref, p_even, p_odd, s_scr):
    @pl.when(pl.program_id(1) == 0)
    def _():
        _load_weight_cols((wq_ref, wk_ref, wv_ref, wg_ref), w16_ref)

    width = wq_ref.shape[0]
    heads = width // HEAD_DIM
    s_scr[...] = jnp.zeros_like(s_scr)

    def chunk(p, r0, n, carry):
        rows = pl.ds(pl.multiple_of(n * CHUNK, CHUNK), CHUNK)
        loc = slice(r0, r0 + CHUNK)
        cos2 = cos_ref[rows, :]
        sin2 = sin_ref[rows, :]
        for hh in range(heads):
            sl = slice(hh * HEAD_DIM, (hh + 1) * HEAD_DIM)
            part = lambda i: p[loc, i * width + hh * HEAD_DIM:i * width + (hh + 1) * HEAD_DIM]
            qr = _rope(part(0), cos2, sin2)
            kr = _rope(part(1), cos2, sin2) * (HEAD_DIM ** -0.5)
            qb = qr.astype(BF16)
            vb = part(2).astype(BF16)
            scores = _dot_nt(qb, kr.astype(BF16)) * dmat_ref[hh]
            s_old = s_scr[hh]
            o = _dot(scores.astype(BF16), vb) + _dot(qb, s_old.astype(BF16)) * qdec_ref[hh]
            kd_t = jnp.transpose(kr * kdec_ref[hh]).astype(BF16)
            s_scr[hh] = cdec_ref[hh] * s_old + _dot(kd_t, vb)
            o_ref[rows, sl] = _head_norm_gate(o, gret_ref[:, sl], part(3)).astype(o_ref.dtype)
        return carry

    _projected_chunks(h_ref, w16_ref, p_even, p_odd, chunk, 0)
    st_ref[0] = s_scr[...]


def _ret_prompt(h, w_t, layer, batch, seq, cos2, sin2, tabs, g_ret):
    hp = 4
    width = hp * HEAD_DIM
    wrow = lambda off: pl.BlockSpec((None, width, D_MODEL), lambda c, b, off=off: (layer, off // width + c, 0),
                                    pipeline_mode=pl.Buffered(1))
    tab = pl.BlockSpec((hp, CHUNK, CHUNK), lambda c, b: (c, 0, 0))
    full = pl.BlockSpec((seq, HEAD_DIM), lambda c, b: (0, 0))
    return pl.pallas_call(
        _ret_prompt_kernel,
        out_shape=(jax.ShapeDtypeStruct((batch * seq, RET_DIM), BF16),
                   jax.ShapeDtypeStruct((batch, RET_HEADS, HEAD_DIM, HEAD_DIM), F32)),
        grid=(RET_HEADS // hp, batch),
        in_specs=[pl.BlockSpec((seq, D_MODEL), lambda c, b: (b, 0), pipeline_mode=pl.Buffered(1)),
                  wrow(OFF_Q), wrow(OFF_K), wrow(OFF_V), wrow(OFF_GR), full, full, tab, tab, tab, tab,
                  pl.BlockSpec((1, width), lambda c, b: (0, c))],
        out_specs=(pl.BlockSpec((seq, width), lambda c, b: (b, c)),
                   pl.BlockSpec((1, hp, HEAD_DIM, HEAD_DIM), lambda c, b: (b, c, 0, 0))),
        scratch_shapes=[pltpu.VMEM((D_MODEL, 4 * width), BF16),
                        pltpu.VMEM((PROJ_ROWS, 4 * width), F32), pltpu.VMEM((PROJ_ROWS, 4 * width), F32),
                        pltpu.VMEM((hp, HEAD_DIM, HEAD_DIM), F32)],
        compiler_params=_params("parallel", "arbitrary"),
        name="ret_prompt",
    )(h, w_t, w_t, w_t, w_t, cos2, sin2, *tabs, g_ret)


def _ret_decode_kernel(q_ref, k_ref, v_ref, g_ref, cos_ref, sin_ref, gam_ref, gret_ref, s_ref, o_ref, sn_ref):
    bt = q_ref.shape[0]
    assert 2 * bt == ITEMS
    cos2 = cos_ref[...]
    sin2 = sin_ref[...]
    lane = lax.broadcasted_iota(jnp.int32, (LANES, LANES), 1)
    zero = jnp.zeros((bt, HEAD_DIM), F32)
    one = jnp.ones((bt, HEAD_DIM), F32)
    for hp in range(RET_HEADS // 2):
        qr, kr, v, gam, sls = [], [], [], [], []
        for hh in range(2):
            h = 2 * hp + hh
            sl = slice(h * HEAD_DIM, (h + 1) * HEAD_DIM)
            sls.append(sl)
            gam.append(gam_ref[h][0:1, :])
            qr.append(_rope(q_ref[:, sl], cos2, sin2))
            kr.append(_rope(k_ref[:, sl], cos2, sin2) * (HEAD_DIM ** -0.5))
            v.append(v_ref[:, sl])
        k2 = [_bf16_parts(t, 2) for t in kr]
        q3 = [_bf16_parts(t, 3) for t in qr]
        v2 = [_bf16_parts(t, 2) for t in v]
        both = lambda parts, n: [parts[0][n], parts[1][n]]
        a = jnp.concatenate(both(k2, 0) + both(k2, 0) + both(k2, 1) + both(q3, 0) + both(q3, 1) + both(q3, 2)
                            + [zero] * 4, axis=0)
        cols = jnp.transpose(a)
        r_kv = jnp.concatenate(both(v2, 0) + both(v2, 1) + both(v2, 0) + [zero] * 10, axis=0)
        r_q = jnp.concatenate([zero] * 6 + [one] * 6 + [zero] * 4, axis=0)
        r = jnp.concatenate([r_kv, r_q], axis=1).astype(BF16)
        q_s = [[], []]
        for i in range(ITEMS):
            hh, j = divmod(i, bt)
            h = 2 * hp + hh
            out = _dot(jnp.where((lane & (ITEMS - 1)) == i, cols, 0.0).astype(BF16), r)
            s_old = s_ref[j, h]
            sn_ref[j, h] = gam[hh] * s_old + out[:, :HEAD_DIM]
            q_s[hh].append(jnp.sum(s_old * out[:, HEAD_DIM:], axis=0, keepdims=True))
        for hh in range(2):
            qk = jnp.sum(qr[hh] * kr[hh], axis=-1, keepdims=True)
            o = qk * v[hh] + jnp.concatenate(q_s[hh], axis=0) * gam[hh]
            o_ref[:, sls[hh]] = _head_norm_gate(o, gret_ref[:, sls[hh]], g_ref[:, sls[hh]]).astype(o_ref.dtype)


def _ret_decode(proj, states, layer, cos2, sin2, gam_tab, g_ret, bt):
    nb = proj.shape[0]
    col = lambda off: pl.BlockSpec((bt, RET_DIM), lambda i, off=off: (i, off // RET_DIM))
    st = pl.BlockSpec((bt, RET_HEADS, HEAD_DIM, HEAD_DIM), lambda i: (i, 0, 0, 0))
    st_in = pl.BlockSpec((None, bt, RET_HEADS, HEAD_DIM, HEAD_DIM), lambda i: (layer, i, 0, 0, 0))
    return pl.pallas_call(
        _ret_decode_kernel,
        out_shape=(jax.ShapeDtypeStruct((nb, RET_DIM), BF16), jax.ShapeDtypeStruct(states.shape[1:], F32)),
        grid=(nb // bt,),
        in_specs=[col(OFF_Q), col(OFF_K), col(OFF_V), col(OFF_GR),
                  pl.BlockSpec((1, HEAD_DIM), lambda i: (0, 0)), pl.BlockSpec((1, HEAD_DIM), lambda i: (0, 0)),
                  pl.BlockSpec((RET_HEADS, SUBLANES, LANES), lambda i: (0, 0, 0)),
                  pl.BlockSpec((1, RET_DIM), lambda i: (0, 0)), st_in],
        out_specs=(pl.BlockSpec((bt, RET_DIM), lambda i: (i, 0)), st),
        compiler_params=_params("parallel"),
        name="ret_decode",
    )(proj, proj, proj, proj, cos2, sin2, gam_tab, g_ret, states)


def _sc_prompt_kernel(h_ref, wbg_ref, wcg_ref, wsh_ref, wgs_ref, w_ref, b_ref, o_ref, st_ref, w16_ref, p_even,
                      p_odd):
    @pl.when(pl.program_id(1) == 0)
    def _():
        _load_weight_cols((wbg_ref, wcg_ref, wsh_ref, wgs_ref), w16_ref)

    width = wbg_ref.shape[0]

    def chunk(p, r0, n, tail):
        rows = pl.ds(pl.multiple_of(n * CHUNK, CHUNK), CHUNK)
        part = lambda i: p[r0:r0 + CHUNK, i * width:(i + 1) * width]
        u = part(1) * part(2)
        conv = _causal_conv_rows(tail, u, w_ref, b_ref)
        o_ref[rows, :] = (part(0) * conv * _silu(part(3))).astype(o_ref.dtype)
        return u[CHUNK - SUBLANES:, :]

    tail = _projected_chunks(h_ref, w16_ref, p_even, p_odd, chunk, jnp.zeros((SUBLANES, width), F32))
    st_ref[0] = tail[SUBLANES - (SC_WIDTH - 1):, :]


def _sc_prompt(h, w_t, layer, batch, seq, w, b):
    width = 512
    wrow = lambda off: pl.BlockSpec((None, width, D_MODEL), lambda c, bi, off=off: (layer, off // width + c, 0),
                                    pipeline_mode=pl.Buffered(1))
    return pl.pallas_call(
        _sc_prompt_kernel,
        out_shape=(jax.ShapeDtypeStruct((batch * seq, SC_DIM), BF16),
                   jax.ShapeDtypeStruct((batch, SC_WIDTH - 1, SC_DIM), F32)),
        grid=(SC_DIM // width, batch),
        in_specs=[pl.BlockSpec((seq, D_MODEL), lambda c, bi: (bi, 0), pipeline_mode=pl.Buffered(1)),
                  wrow(OFF_BG), wrow(OFF_CG), wrow(OFF_SH), wrow(OFF_GS),
                  pl.BlockSpec((SC_WIDTH, width), lambda c, bi: (0, c)),
                  pl.BlockSpec((1, width), lambda c, bi: (0, c))],
        out_specs=(pl.BlockSpec((seq, width), lambda c, bi: (bi, c)),
                   pl.BlockSpec((1, SC_WIDTH - 1, width), lambda c, bi: (bi, 0, c))),
        scratch_shapes=[pltpu.VMEM((D_MODEL, 4 * width), BF16),
                        pltpu.VMEM((PROJ_ROWS, 4 * width), F32), pltpu.VMEM((PROJ_ROWS, 4 * width), F32)],
        compiler_params=_params("parallel", "arbitrary"),
        name="sc_prompt",
    )(h, w_t, w_t, w_t, w_t, w, b)


def _sc_decode_kernel(bg_ref, cg_ref, sh_ref, gs_ref, st_ref, w_ref, b_ref, o_ref, stn_ref):
    u = cg_ref[...] * sh_ref[...]
    r0 = st_ref[:, 0:SC_DIM]
    r1 = st_ref[:, SC_DIM:]
    conv = b_ref[...] + r0 * w_ref[0:1, :] + r1 * w_ref[1:2, :] + u * w_ref[2:3, :]
    o_ref[...] = (bg_ref[...] * conv * _silu(gs_ref[...])).astype(o_ref.dtype)
    stn_ref[:, 0:SC_DIM] = r1
    stn_ref[:, SC_DIM:] = u


def _sc_decode(proj, state2d, w, b):
    nb = proj.shape[0]
    col = lambda off: pl.BlockSpec((nb, SC_DIM), lambda i, off=off: (0, off // SC_DIM))
    whole = lambda shape: pl.BlockSpec(shape, lambda i: (0,) * len(shape))
    return pl.pallas_call(
        _sc_decode_kernel,
        out_shape=(jax.ShapeDtypeStruct((nb, SC_DIM), BF16), jax.ShapeDtypeStruct(state2d.shape, F32)),
        grid=(1,),
        in_specs=[col(OFF_BG), col(OFF_CG), col(OFF_SH), col(OFF_GS), whole(state2d.shape),
                  whole((SC_WIDTH, SC_DIM)), whole((1, SC_DIM))],
        out_specs=(whole((nb, SC_DIM)), whole(state2d.shape)),
        compiler_params=_params("arbitrary"),
        name="sc_decode",
    )(proj, proj, proj, proj, state2d, w, b)


def _lane_cumsum(a, lane):
    s = 1
    while s < CHUNK:
        a = a + jnp.where(lane >= s, pltpu.roll(a, s, 1), 0.0)
        s *= 2
    return a


def _ssd_prompt_kernel(h_ref, wz_ref, wx_ref, wb_ref, wc_ref, wdt_ref, dtb_ref, alog_ref, dsk_ref,
                       cwx_ref, cwb_ref, cwc_ref, cbx_ref, cbb_ref, cbc_ref,
                       y_ref, stx_ref, stb_ref, stc_ref, sst_ref, w16_ref, p_even, p_odd, s_scr):
    @pl.when(pl.program_id(1) == 0)
    def _():
        _load_weight_cols((wz_ref, wx_ref, wb_ref, wc_ref, wdt_ref), w16_ref)

    gw = wz_ref.shape[0]
    n_st = wb_ref.shape[0]
    off_x, off_b, off_c, off_dt = gw, 2 * gw, 2 * gw + n_st, 2 * gw + 2 * n_st
    row = lax.broadcasted_iota(jnp.int32, (CHUNK, CHUNK), 0)
    lane = lax.broadcasted_iota(jnp.int32, (CHUNK, CHUNK), 1)
    lane8 = lax.broadcasted_iota(jnp.int32, (HEADS_PER_GROUP, CHUNK), 1)
    causal = row >= lane
    low_lanes = lane < SSM_HEAD_DIM
    low_rows = row < SSM_HEAD_DIM
    a_neg = -jnp.exp(alog_ref[0])
    dt_bias = dtb_ref[0]
    pad_heads = jnp.zeros((CHUNK - HEADS_PER_GROUP, CHUNK), F32)
    s_scr[...] = jnp.zeros_like(s_scr)

    def chunk(p, r0, n, tails):
        tx, tb, tc = tails
        rows = pl.ds(pl.multiple_of(n * CHUNK, CHUNK), CHUNK)
        loc = slice(r0, r0 + CHUNK)
        x_raw = p[loc, off_x:off_x + gw]
        b_raw = p[loc, off_b:off_b + n_st]
        c_raw = p[loc, off_c:off_c + n_st]
        xc = _silu(_causal_conv_rows(tx, x_raw, cwx_ref, cbx_ref))
        bb = _silu(_causal_conv_rows(tb, b_raw, cwb_ref, cbb_ref)).astype(BF16)
        cb16 = _silu(_causal_conv_rows(tc, c_raw, cwc_ref, cbc_ref)).astype(BF16)
        dt_t = _softplus(jnp.transpose(p[loc, off_dt:off_dt + LANES])[0:HEADS_PER_GROUP, :] + dt_bias)
        acum_t = _lane_cumsum(dt_t * a_neg, lane8)
        acum = jnp.transpose(jnp.concatenate([acum_t, pad_heads], axis=0))
        cb = _dot_nt(cb16, bb)
        for pp in range(PAIRS_PER_GROUP):
            sl = slice(pp * PAIR, (pp + 1) * PAIR)
            x_pair = xc[:, sl]
            m, e_col, w_row, dec = [], [], [], []
            for hh in range(2):
                c = 2 * pp + hh
                a_col = jnp.broadcast_to(acum[:, c:c + 1], (CHUNK, CHUNK))
                a_row = acum_t[c:c + 1, :]
                dt_row = dt_t[c:c + 1, :]
                lmat = jnp.exp(jnp.where(causal, a_col - a_row, -jnp.inf))
                m.append((cb * lmat * dt_row).astype(BF16))
                e_col.append(jnp.exp(a_col))
                last = a_col[CHUNK - 1:CHUNK, :]
                w_row.append(jnp.exp(last - a_row) * dt_row)
                dec.append(jnp.exp(last))
            x_lo = jnp.where(low_lanes, x_pair, 0.0).astype(BF16)
            x_hi = jnp.where(low_lanes, 0.0, x_pair).astype(BF16)
            s_old = s_scr[pp]
            y = _dot(m[0], x_lo) + _dot(m[1], x_hi)
            y = y + _dot_nt(cb16, s_old.astype(BF16)) * jnp.where(low_lanes, e_col[0], e_col[1])
            y = y + dsk_ref[0][:, sl] * x_pair
            y_ref[rows, sl] = y * _silu(p[loc, pp * PAIR:(pp + 1) * PAIR])
            xw = (jnp.transpose(x_pair) * jnp.where(low_rows, w_row[0], w_row[1])).astype(BF16)
            s_scr[pp] = jnp.where(low_rows, dec[0], dec[1]) * s_old + _dot(xw, bb)
        cut = CHUNK - SUBLANES
        return x_raw[cut:, :], b_raw[cut:, :], c_raw[cut:, :]

    zeros = lambda width: jnp.zeros((SUBLANES, width), F32)
    tx, tb, tc = _projected_chunks(h_ref, w16_ref, p_even, p_odd, chunk, (zeros(gw), zeros(n_st), zeros(n_st)))
    keep = SUBLANES - (SSM_CONV - 1)
    stx_ref[0] = tx[keep:, :]
    stb_ref[0] = tb[keep:, :]
    stc_ref[0] = tc[keep:, :]
    for pp in range(PAIRS_PER_GROUP):
        for hh in range(2):
            sst_ref[0, 2 * pp + hh] = s_scr[pp, hh * SSM_HEAD_DIM:(hh + 1) * SSM_HEAD_DIM, :]


def _ssd_prompt(h, w_t, layer, batch, seq, dtb, alog, dsk, conv_w, conv_b):
    gw = SSM_DIM // SSM_GROUPS
    n = SSM_STATE
    wrow = lambda off, width: pl.BlockSpec((None, width, D_MODEL),
                                           lambda g, b, off=off, width=width: (layer, off // width + g, 0),
                                           pipeline_mode=pl.Buffered(1))
    grp = lambda rows, width: pl.BlockSpec((1, rows, width), lambda g, b: (g, 0, 0))
    cw = lambda off, width: pl.BlockSpec((SSM_CONV, width), lambda g, b, off=off, width=width: (0, off // width + g))
    cbias = lambda off, width: pl.BlockSpec((1, width), lambda g, b, off=off, width=width: (0, off // width + g))
    st = lambda width: pl.BlockSpec((1, SSM_CONV - 1, width), lambda g, b: (b, 0, g))
    n_proj = 2 * gw + 2 * n + LANES
    return pl.pallas_call(
        _ssd_prompt_kernel,
        out_shape=(jax.ShapeDtypeStruct((batch * seq, SSM_DIM), F32),
                   jax.ShapeDtypeStruct((batch, SSM_CONV - 1, SSM_DIM), F32),
                   jax.ShapeDtypeStruct((batch, SSM_CONV - 1, SSM_GROUPS * n), F32),
                   jax.ShapeDtypeStruct((batch, SSM_CONV - 1, SSM_GROUPS * n), F32),
                   jax.ShapeDtypeStruct((batch, SSM_HEADS, SSM_HEAD_DIM, n), F32)),
        grid=(SSM_GROUPS, batch),
        in_specs=[pl.BlockSpec((seq, D_MODEL), lambda g, b: (b, 0)),
                  wrow(OFF_Z, gw), wrow(OFF_X, gw), wrow(OFF_B, n), wrow(OFF_C, n), wrow(OFF_DT, HEADS_PER_GROUP),
                  grp(HEADS_PER_GROUP, CHUNK), grp(HEADS_PER_GROUP, CHUNK), grp(1, gw),
                  cw(0, gw), cw(SSM_DIM, n), cw(SSM_DIM + SSM_GROUPS * n, n),
                  cbias(0, gw), cbias(SSM_DIM, n), cbias(SSM_DIM + SSM_GROUPS * n, n)],
        out_specs=(pl.BlockSpec((seq, gw), lambda g, b: (b, g)), st(gw), st(n), st(n),
                   pl.BlockSpec((1, HEADS_PER_GROUP, SSM_HEAD_DIM, n), lambda g, b: (b, g, 0, 0))),
        scratch_shapes=[pltpu.VMEM((D_MODEL, n_proj), BF16),
                        pltpu.VMEM((PROJ_ROWS, n_proj), F32), pltpu.VMEM((PROJ_ROWS, n_proj), F32),
                        pltpu.VMEM((PAIRS_PER_GROUP, PAIR, n), F32)],
        compiler_params=_params("parallel", "arbitrary"),
        name="ssd_prompt",
    )(h, w_t, w_t, w_t, w_t, w_t, dtb, alog, dsk, conv_w, conv_w, conv_w, conv_b, conv_b, conv_b)


def _ssd_decode_kernel(z_ref, x_ref, bc_ref, dtr_ref, cst_ref, s_ref, cw_ref, cb_ref, dtb_ref, alog_ref, dsk_ref,
                       y_ref, cstn_ref, sn_ref):
    bt = z_ref.shape[0]
    xr = SSM_DIM // LANES
    gr = SSM_GROUPS * SSM_STATE // LANES
    u = jnp.concatenate([x_ref[...], bc_ref[...]], axis=1)
    conv = cb_ref[...]
    for j in range(SSM_CONV - 1):
        conv = conv + cst_ref[:, j] * cw_ref[j]
        cstn_ref[:, j] = u if j == SSM_CONV - 2 else cst_ref[:, j + 1]
    act = _silu(conv + u * cw_ref[SSM_CONV - 1])
    xs = act[:, 0:xr]
    bm = act[:, xr:xr + gr]
    cm = act[:, xr + gr:]
    dt = _softplus(dtr_ref[...] + dtb_ref[...])
    ea = jnp.exp(dt * -jnp.exp(alog_ref[...]))
    xdt = xs * dt
    assert xr == ITEMS == N_PAIRS
    lane = lax.broadcasted_iota(jnp.int32, (LANES, LANES), 1)
    row16 = lax.broadcasted_iota(jnp.int32, (xr, LANES), 0)
    zeros = lambda n: jnp.zeros((n, LANES), F32)
    r_ea = jnp.concatenate([zeros(3 * ITEMS), jnp.ones((3 * ITEMS, LANES), F32), zeros(2 * ITEMS)], axis=0)
    for j in range(bt):
        cbt = zeros(xr)
        b_rows = zeros(xr)
        for g in range(SSM_GROUPS):
            in_group = row16 // PAIRS_PER_GROUP == g
            cb_g = jnp.sum(cm[j, g:g + 1, :] * bm[j, g:g + 1, :], axis=-1, keepdims=True)
            cbt = jnp.where(in_group, cb_g, cbt)
            b_rows = jnp.where(in_group, bm[j, g:g + 1, :], b_rows)
        xd2 = _bf16_parts(xdt[j], 2)
        ea3 = _bf16_parts(ea[j], 3)
        b2 = _bf16_parts(b_rows, 2)
        a = jnp.concatenate([xd2[0], xd2[0], xd2[1]] + ea3 + [zeros(2 * ITEMS)], axis=0)
        cols = jnp.transpose(a)
        r_xb = jnp.concatenate([b2[0], b2[1], b2[0], zeros(5 * ITEMS)], axis=0)
        r = jnp.concatenate([r_xb, r_ea], axis=1).astype(BF16)
        y_t = jnp.zeros((LANES, LANES), F32)
        for pp in range(N_PAIRS):
            g = pp // PAIRS_PER_GROUP
            out = _dot(jnp.where((lane & (ITEMS - 1)) == pp, cols, 0.0).astype(BF16), r)
            s_old = s_ref[j, 2 * pp:2 * pp + 2].reshape(PAIR, SSM_STATE)
            y_col = jnp.sum(s_old * cm[j, g:g + 1, :], axis=-1, keepdims=True)
            y_t = jnp.where(lane == pp, y_col, y_t)
            s_new = out[:, SSM_STATE:] * s_old + out[:, :SSM_STATE]
            sn_ref[j, 2 * pp:2 * pp + 2] = s_new.reshape(2, SSM_HEAD_DIM, SSM_STATE)
        y_inter = jnp.transpose(y_t)[0:xr, :]
        y = cbt * xdt[j] + y_inter * ea[j] + dsk_ref[0] * xs[j]
        y_ref[j] = y * _silu(z_ref[j])


def _ssd_decode(proj3, dt_raw3, cstates, states, layer, conv_w3, conv_b3, dtb3, alog3, dsk3, bt):
    nb = proj3.shape[0]
    xr = SSM_DIM // LANES
    cr = SSM_CONV_DIM // LANES
    whole = lambda shape: pl.BlockSpec(shape, lambda i: (0,) * len(shape))
    st = pl.BlockSpec((bt, SSM_HEADS, SSM_HEAD_DIM, SSM_STATE), lambda i: (i, 0, 0, 0))
    cst = pl.BlockSpec((bt, SSM_CONV - 1, cr, LANES), lambda i: (i, 0, 0, 0))
    st_in = pl.BlockSpec((None, bt, SSM_HEADS, SSM_HEAD_DIM, SSM_STATE), lambda i: (layer, i, 0, 0, 0))
    cst_in = pl.BlockSpec((None, bt, SSM_CONV - 1, cr, LANES), lambda i: (layer, i, 0, 0, 0))
    return pl.pallas_call(
        _ssd_decode_kernel,
        out_shape=(jax.ShapeDtypeStruct((nb, xr, LANES), F32), jax.ShapeDtypeStruct(cstates.shape[1:], F32),
                   jax.ShapeDtypeStruct(states.shape[1:], F32)),
        grid=(nb // bt,),
        in_specs=[pl.BlockSpec((bt, xr, LANES), lambda i: (i, OFF_Z // SSM_DIM, 0)),
                  pl.BlockSpec((bt, xr, LANES), lambda i: (i, OFF_X // SSM_DIM, 0)),
                  pl.BlockSpec((bt, SUBLANES, LANES), lambda i: (i, OFF_B // (SUBLANES * LANES), 0)),
                  pl.BlockSpec((bt, xr, LANES), lambda i: (i, 0, 0)),
                  cst_in, st_in, whole((SSM_CONV, cr, LANES)), whole((1, cr, LANES)),
                  whole((1, xr, LANES)), whole((1, xr, LANES)), whole((1, xr, LANES))],
        out_specs=(pl.BlockSpec((bt, xr, LANES), lambda i: (i, 0, 0)), cst, st),
        compiler_params=_params("parallel"),
        name="ssd_decode",
    )(proj3, proj3, proj3, dt_raw3, cstates, states, conv_w3, conv_b3, dtb3, alog3, dsk3)


def _rope_tables(pos):
    half = HEAD_DIM // 2
    inv = ROPE_BASE ** (-jnp.arange(half, dtype=F32) / half)
    ang = pos.astype(F32)[:, None] * inv[None, :]
    cos, sin = jnp.cos(ang), jnp.sin(ang)
    return jnp.concatenate([cos, cos], axis=-1), jnp.concatenate([-sin, sin], axis=-1)


def _retention_tables():
    lg = jnp.log(1.0 - 2.0 ** (-5.0 - jnp.arange(RET_HEADS, dtype=F32)))
    i = jnp.arange(CHUNK, dtype=F32)
    diff = i[:, None] - i[None, :]
    dmat = jnp.exp(jnp.where((diff >= 0)[None], diff[None] * lg[:, None, None], -jnp.inf))
    full = lambda t: jnp.broadcast_to(t, (RET_HEADS, CHUNK, CHUNK))
    q_dec = full(jnp.exp((i + 1.0)[None, :, None] * lg[:, None, None]))
    k_dec = full(jnp.exp((CHUNK - 1.0 - i)[None, :, None] * lg[:, None, None]))
    c_dec = full(jnp.exp(CHUNK * lg)[:, None, None])
    gam = jnp.broadcast_to(jnp.exp(lg)[:, None, None], (RET_HEADS, SUBLANES, LANES))
    return (dmat, q_dec, k_dec, c_dec), gam


def _per_group(v):
    v = v.astype(F32).reshape(SSM_GROUPS, HEADS_PER_GROUP, 1)
    return jnp.broadcast_to(v, (SSM_GROUPS, HEADS_PER_GROUP, CHUNK))


def _per_lane(v):
    return jnp.repeat(v.astype(F32), SSM_HEAD_DIM)[None, :]


def kernel(x_prompt, x_sample, state_ret, state_sconv, state_ssm_conv, state_ssm, w_in, w_out, norm_pre, norm_post,
           ret_norm, sc_conv_w, sc_conv_b, ssm_conv_w, ssm_conv_b, ssm_dt_bias, ssm_a_log, ssm_d, ssm_norm):
    batch, seq, _ = x_prompt.shape
    nb = x_sample.shape[0]
    depth = w_in.shape[0]
    mp = batch * seq
    xr = SSM_DIM // LANES
    cr = SSM_CONV_DIM // LANES

    cos_p, sin_p = _rope_tables(jnp.arange(seq))
    cos_s, sin_s = _rope_tables(PAST_LEN + jnp.arange(1))
    ret_tabs, gam_tab = _retention_tables()

    w_out16 = w_out.astype(BF16)
    w_t = jnp.swapaxes(w_in, 1, 2)
    w_dt_lane = jnp.repeat(w_t[:, OFF_DT:, :], SSM_HEAD_DIM, axis=1).astype(BF16)

    tm_o = min(256, mp)
    xp = x_prompt.reshape(mp, D_MODEL)
    xs = x_sample.reshape(nb, D_MODEL)
    hp = _prenorm(xp, norm_pre[0][None, :], tm_o)
    hs = _prenorm(xs, norm_pre[0][None, :], nb)

    outs = [[] for _ in range(8)]
    for l in range(depth):
        g_next = norm_pre[(l + 1) % depth][None, :]
        dtb_g, alog_g = _per_group(ssm_dt_bias[l]), _per_group(ssm_a_log[l])
        dsk_lane = _per_lane(ssm_d[l])
        dsk_g = dsk_lane.reshape(SSM_GROUPS, 1, SSM_DIM // SSM_GROUPS)
        conv_b = ssm_conv_b[l][None, :]

        o_ret, r_new = _ret_prompt(hp, w_t, l, batch, seq, cos_p, sin_p, ret_tabs, ret_norm[l][None, :])
        o_sc, c_new = _sc_prompt(hp, w_t, l, batch, seq, sc_conv_w[l], sc_conv_b[l][None, :])
        ssm_pre, cx, cb_, cc, s_new = _ssd_prompt(hp, w_t, l, batch, seq, dtb_g, alog_g, dsk_g, ssm_conv_w[l], conv_b)
        xp, hp = _outproj(o_ret, o_sc, ssm_pre, xp, w_out16, l, ssm_norm[l][None, :], norm_post[l][None, :],
                          g_next, tm_o)
        outs[0].append(r_new)
        outs[1].append(c_new)
        outs[2].append(jnp.concatenate([cx, cb_, cc], axis=-1))
        outs[3].append(s_new)

        proj_s = _matmul_nt(hs, w_t, l, N_MAIN, 1024, "inproj_decode")
        dt_s = _matmul_nt(hs, w_dt_lane, l, SSM_DIM, SSM_DIM, "dtproj_decode")
        o_ret, r_new = _ret_decode(proj_s, state_ret, l, cos_s, sin_s, gam_tab, ret_norm[l][None, :], SUBLANES)
        o_sc, c_new = _sc_decode(proj_s, state_sconv[l].reshape(nb, (SC_WIDTH - 1) * SC_DIM), sc_conv_w[l],
                                 sc_conv_b[l][None, :])
        y_s, cs_new, s_new = _ssd_decode(
            proj_s.reshape(nb, N_MAIN // LANES, LANES), dt_s.reshape(nb, xr, LANES),
            state_ssm_conv.reshape(depth, nb, SSM_CONV - 1, cr, LANES), state_ssm, l,
            ssm_conv_w[l].reshape(SSM_CONV, cr, LANES), conv_b.reshape(1, cr, LANES),
            _per_lane(ssm_dt_bias[l]).reshape(1, xr, LANES), _per_lane(ssm_a_log[l]).reshape(1, xr, LANES),
            dsk_lane.reshape(1, xr, LANES), 4)
        xs, hs = _outproj(o_ret, o_sc, y_s.reshape(nb, SSM_DIM), xs, w_out16, l, ssm_norm[l][None, :],
                          norm_post[l][None, :], g_next, nb)
        outs[4].append(r_new)
        outs[5].append(c_new.reshape(nb, SC_WIDTH - 1, SC_DIM))
        outs[6].append(cs_new.reshape(nb, SSM_CONV - 1, SSM_CONV_DIM))
        outs[7].append(s_new)

    stacked = [jnp.stack(o) for o in outs]
    return (xp.reshape(batch, seq, D_MODEL), xs.reshape(nb, 1, D_MODEL), *stacked)
```

```python
import functools

import jax
import jax.numpy as jnp
import numpy as np
from jax import lax
from jax.experimental import pallas as pl
from jax.experimental.pallas import tpu as pltpu

F32 = jnp.float32
BF16 = jnp.bfloat16

D_MODEL = 2048
D_MIX = 2 * D_MODEL
RET_HEADS = 8
RET_DIM = 1024
HEAD_DIM = 128
SC_DIM = 1024
SC_WIDTH = 3
SSM_DIM = 2048
SSM_HEAD_DIM = 64
SSM_HEADS = 32
SSM_GROUPS = 4
SSM_STATE = 128
SSM_CONV = 4
SSM_CONV_DIM = SSM_DIM + 2 * SSM_GROUPS * SSM_STATE
CHUNK = 128
PROJ_ROWS = 2 * CHUNK
ROPE_BASE = 10000.0
EPS = 1e-6
PAST_LEN = 16384

OFF_Q, OFF_K, OFF_V, OFF_GR = 0, 1024, 2048, 3072
OFF_BG, OFF_CG, OFF_SH, OFF_GS = 4096, 5120, 6144, 7168
OFF_Z, OFF_X, OFF_B, OFF_C, OFF_DT = 8192, 10240, 12288, 12800, 13312
N_MAIN = OFF_DT

LANES = 128
SUBLANES = 8
V7X_VMEM_LIMIT = 60 * 1024 * 1024
PAIR = 2 * SSM_HEAD_DIM
HEADS_PER_GROUP = SSM_HEADS // SSM_GROUPS
PAIRS_PER_GROUP = HEADS_PER_GROUP // 2
N_PAIRS = SSM_HEADS // 2
ITEMS = 16


def _params(*sem):
    return pltpu.CompilerParams(dimension_semantics=sem, vmem_limit_bytes=V7X_VMEM_LIMIT)


def _silu(x):
    return x * jax.nn.sigmoid(x)


def _softplus(x):
    return jnp.maximum(x, 0.0) + jnp.log1p(jnp.exp(-jnp.abs(x)))


def _dot(a, b):
    return jnp.dot(a, b, preferred_element_type=F32)


def _bf16_parts(x, n):
    parts = []
    for _ in range(n):
        p = x.astype(BF16).astype(F32)
        parts.append(p)
        x = x - p
    return parts


def _dot_nt(a, b):
    return lax.dot_general(a, b, (((1,), (1,)), ((), ())), preferred_element_type=F32)


def _prenorm_kernel(x_ref, g_ref, o_ref):
    x = x_ref[...]
    ms = jnp.mean(x * x, axis=-1, keepdims=True)
    o_ref[...] = (x * lax.rsqrt(ms + EPS) * g_ref[...]).astype(o_ref.dtype)


def _prenorm(x, g, tm):
    m, d = x.shape
    return pl.pallas_call(
        _prenorm_kernel,
        out_shape=jax.ShapeDtypeStruct((m, d), BF16),
        grid=(m // tm,),
        in_specs=[pl.BlockSpec((tm, d), lambda i: (i, 0)), pl.BlockSpec((1, d), lambda i: (0, 0))],
        out_specs=pl.BlockSpec((tm, d), lambda i: (i, 0)),
        compiler_params=_params("parallel"),
        name="prenorm",
    )(x, g)


def _mm_nt_kernel(x_ref, w_ref, o_ref):
    o_ref[...] = _dot_nt(x_ref[...], w_ref[...].astype(BF16))


def _matmul_nt(x, w_t, layer, n_cols, tn, name):
    m, k = x.shape
    return pl.pallas_call(
        _mm_nt_kernel,
        out_shape=jax.ShapeDtypeStruct((m, n_cols), F32),
        grid=(n_cols // tn,),
        in_specs=[pl.BlockSpec((m, k), lambda j: (0, 0)),
                  pl.BlockSpec((None, tn, k), lambda j: (layer, j, 0))],
        out_specs=pl.BlockSpec((m, tn), lambda j: (0, j)),
        compiler_params=_params("parallel"),
        name=name,
    )(x, w_t)


def _outproj_kernel(oret_ref, osc_ref, ssm_ref, x_ref, w_ref, gssm_ref, gpost_ref, gnext_ref, y_ref, hn_ref):
    ypre = ssm_ref[...]
    ms = jnp.mean(ypre * ypre, axis=-1, keepdims=True)
    ossm = (ypre * lax.rsqrt(ms + EPS) * gssm_ref[...]).astype(BF16)
    acc = _dot(oret_ref[...], w_ref[0:RET_DIM, :])
    acc = acc + _dot(osc_ref[...], w_ref[RET_DIM:RET_DIM + SC_DIM, :])
    acc = acc + _dot(ossm, w_ref[RET_DIM + SC_DIM:, :])
    ms2 = jnp.mean(acc * acc, axis=-1, keepdims=True)
    y = x_ref[...] + acc * lax.rsqrt(ms2 + EPS) * gpost_ref[...]
    y_ref[...] = y
    ms3 = jnp.mean(y * y, axis=-1, keepdims=True)
    hn_ref[...] = (y * lax.rsqrt(ms3 + EPS) * gnext_ref[...]).astype(hn_ref.dtype)


def _outproj(oret, osc, ssm_pre, x, w_out, layer, g_ssm, g_post, g_next, tm):
    m = x.shape[0]
    row = lambda width: pl.BlockSpec((tm, width), lambda i: (i, 0))
    vec = lambda width: pl.BlockSpec((1, width), lambda i: (0, 0))
    return pl.pallas_call(
        _outproj_kernel,
        out_shape=(jax.ShapeDtypeStruct((m, D_MODEL), F32), jax.ShapeDtypeStruct((m, D_MODEL), BF16)),
        grid=(m // tm,),
        in_specs=[row(RET_DIM), row(SC_DIM), row(SSM_DIM), row(D_MODEL),
                  pl.BlockSpec((None, D_MIX, D_MODEL), lambda i: (layer, 0, 0), pipeline_mode=pl.Buffered(1)),
                  vec(SSM_DIM), vec(D_MODEL), vec(D_MODEL)],
        out_specs=(row(D_MODEL), row(D_MODEL)),
        compiler_params=_params("parallel"),
        name="outproj",
    )(oret, osc, ssm_pre, x, w_out, g_ssm, g_post, g_next)


def _rope(t, cos2, sin2):
    return t * cos2 + pltpu.roll(t, HEAD_DIM // 2, 1) * sin2


def _head_norm_gate(o, g_row, gate):
    mu = jnp.mean(o, axis=-1, keepdims=True)
    oc = o - mu
    var = jnp.mean(oc * oc, axis=-1, keepdims=True)
    return oc * lax.rsqrt(var + EPS) * g_row * _silu(gate)


def _shift_rows(tail, u, s):
    ext = jnp.concatenate([tail, u], axis=0)
    return ext[SUBLANES - s:SUBLANES - s + u.shape[0]]


def _causal_conv_rows(tail, u, w_ref, b_ref):
    width = w_ref.shape[0]
    out = b_ref[...]
    for j in range(width - 1):
        out = out + _shift_rows(tail, u, width - 1 - j) * w_ref[j:j + 1, :]
    return out + u * w_ref[width - 1:width, :]


def _load_weight_cols(w_refs, w16_ref):
    off = 0
    for w_ref in w_refs:
        n = w_ref.shape[0]
        for r in range(0, n, LANES):
            blk = w_ref[r:min(r + LANES, n), :]
            if blk.shape[0] < LANES:
                blk = jnp.concatenate([blk, jnp.zeros((LANES - blk.shape[0], blk.shape[1]), F32)], axis=0)
            w16_ref[:, off:off + LANES] = jnp.transpose(blk).astype(BF16)
            off += LANES


def _projected_chunks(h_ref, w16_ref, p_even, p_odd, chunk_fn, carry):
    steps = h_ref.shape[0] // PROJ_ROWS
    per_step = PROJ_ROWS // CHUNK

    def project(step, p_ref):
        rows = pl.ds(pl.multiple_of(step * PROJ_ROWS, PROJ_ROWS), PROJ_ROWS)
        p_ref[...] = _dot(h_ref[rows, :], w16_ref[...])

    def mix(step, p_ref, carry):
        for c in range(per_step):
            carry = chunk_fn(p_ref, c * CHUNK, step * per_step + c, carry)
        return carry

    def body(i, carry):
        project(2 * i + 1, p_odd)
        carry = mix(2 * i, p_even, carry)
        project(2 * i + 2, p_even)
        return mix(2 * i + 1, p_odd, carry)

    project(0, p_even)
    full = (steps - 1) // 2
    carry = lax.fori_loop(0, full, body, carry)
    if steps - 2 * full == 2:
        project(2 * full + 1, p_odd)
        carry = mix(2 * full, p_even, carry)
        return mix(2 * full + 1, p_odd, carry)
    return mix(2 * full, p_even, carry)


def _ret_prompt_kernel(h_ref, wq_ref, wk_ref, wv_ref, wg_ref, cos_ref, sin_ref, dmat_ref, qdec_ref, kdec_ref,
                       cdec_ref, gret_ref, o_ref, st_ref, w16_ref, p_even, p_odd, s_scr):
    @pl.when(pl.program_id(1) == 0)
    def _():
        _load_weight_cols((wq_ref, wk_ref, wv_ref, wg_ref), w16_ref)

    width = wq_ref.shape[0]
    heads = width // HEAD_DIM
    s_scr[...] = jnp.zeros_like(s_scr)

    def chunk(p, r0, n, carry):
        rows = pl.ds(pl.multiple_of(n * CHUNK, CHUNK), CHUNK)
        loc = slice(r0, r0 + CHUNK)
        cos2 = cos_ref[rows, :]
        sin2 = sin_ref[rows, :]
        for hh in range(heads):
            sl = slice(hh * HEAD_DIM, (hh + 1) * HEAD_DIM)
            part = lambda i: p[loc, i * width + hh * HEAD_DIM:i * width + (hh + 1) * HEAD_DIM]
            qr = _rope(part(0), cos2, sin2)
            kr = _rope(part(1), cos2, sin2) * (HEAD_DIM ** -0.5)
            qb = qr.astype(BF16)
            vb = part(2).astype(BF16)
            scores = _dot_nt(qb, kr.astype(BF16)) * dmat_ref[hh]
            s_old = s_scr[hh]
            o = _dot(scores.astype(BF16), vb) + _dot(qb, s_old.astype(BF16)) * qdec_ref[hh]
            kd_t = jnp.transpose(kr * kdec_ref[hh]).astype(BF16)
            s_scr[hh] = cdec_ref[hh] * s_old + _dot(kd_t, vb)
            o_ref[rows, sl] = _head_norm_gate(o, gret_ref[:, sl], part(3)).astype(o_ref.dtype)
        return carry

    _projected_chunks(h_ref, w16_ref, p_even, p_odd, chunk, 0)
    st_ref[0] = s_scr[...]


def _ret_prompt(h, w_t, layer, batch, seq, cos2, sin2, tabs, g_ret):
    hp = 4
    width = hp * HEAD_DIM
    wrow = lambda off: pl.BlockSpec((None, width, D_MODEL), lambda c, b, off=off: (layer, off // width + c, 0),
                                    pipeline_mode=pl.Buffered(1))
    tab = pl.BlockSpec((hp, CHUNK, CHUNK), lambda c, b: (c, 0, 0), pipeline_mode=pl.Buffered(1))
    full = pl.BlockSpec((seq, HEAD_DIM), lambda c, b: (0, 0), pipeline_mode=pl.Buffered(1))
    return pl.pallas_call(
        _ret_prompt_kernel,
        out_shape=(jax.ShapeDtypeStruct((batch * seq, RET_DIM), BF16),
                   jax.ShapeDtypeStruct((batch, RET_HEADS, HEAD_DIM, HEAD_DIM), F32)),
        grid=(RET_HEADS // hp, batch),
        in_specs=[pl.BlockSpec((seq, D_MODEL), lambda c, b: (b, 0)),
                  wrow(OFF_Q), wrow(OFF_K), wrow(OFF_V), wrow(OFF_GR), full, full, tab, tab, tab, tab,
                  pl.BlockSpec((1, width), lambda c, b: (0, c))],
        out_specs=(pl.BlockSpec((seq, width), lambda c, b: (b, c)),
                   pl.BlockSpec((1, hp, HEAD_DIM, HEAD_DIM), lambda c, b: (b, c, 0, 0))),
        scratch_shapes=[pltpu.VMEM((D_MODEL, 4 * width), BF16),
                        pltpu.VMEM((PROJ_ROWS, 4 * width), F32), pltpu.VMEM((PROJ_ROWS, 4 * width), F32),
                        pltpu.VMEM((hp, HEAD_DIM, HEAD_DIM), F32)],
        compiler_params=_params("parallel", "arbitrary"),
        name="ret_prompt",
    )(h, w_t, w_t, w_t, w_t, cos2, sin2, *tabs, g_ret)


def _ret_decode_kernel(q_ref, k_ref, v_ref, g_ref, cos_ref, sin_ref, gam_ref, gret_ref, s_ref, acc_ref, o_ref,
                       sn_ref):
    del acc_ref
    bt = q_ref.shape[0]
    assert 2 * bt == ITEMS
    cos2 = cos_ref[...]
    sin2 = sin_ref[...]
    lane = lax.broadcasted_iota(jnp.int32, (LANES, LANES), 1)
    zero = jnp.zeros((bt, HEAD_DIM), F32)
    one = jnp.ones((bt, HEAD_DIM), F32)
    for hp in range(RET_HEADS // 2):
        qr, kr, v, gam, sls = [], [], [], [], []
        for hh in range(2):
            h = 2 * hp + hh
            sl = slice(h * HEAD_DIM, (h + 1) * HEAD_DIM)
            sls.append(sl)
            gam.append(gam_ref[h][0:1, :])
            qr.append(_rope(q_ref[:, sl], cos2, sin2))
            kr.append(_rope(k_ref[:, sl], cos2, sin2) * (HEAD_DIM ** -0.5))
            v.append(v_ref[:, sl])
        k2 = [_bf16_parts(t, 2) for t in kr]
        q3 = [_bf16_parts(t, 3) for t in qr]
        v2 = [_bf16_parts(t, 2) for t in v]
        both = lambda parts, n: [parts[0][n], parts[1][n]]
        a = jnp.concatenate(both(k2, 0) + both(k2, 0) + both(k2, 1) + both(q3, 0) + both(q3, 1) + both(q3, 2)
                            + [zero] * 4, axis=0)
        cols = jnp.transpose(a)
        r_kv = jnp.concatenate(both(v2, 0) + both(v2, 1) + both(v2, 0) + [zero] * 10, axis=0)
        r_q = jnp.concatenate([zero] * 6 + [one] * 6 + [zero] * 4, axis=0)
        r = jnp.concatenate([r_kv, r_q], axis=1).astype(BF16)
        q_s = [[], []]
        for i in range(ITEMS):
            hh, j = divmod(i, bt)
            h = 2 * hp + hh
            out = _dot(jnp.where((lane & (ITEMS - 1)) == i, cols, 0.0).astype(BF16), r)
            s_old = s_ref[j, h]
            sn_ref[j, h] = gam[hh] * s_old + out[:, :HEAD_DIM]
            q_s[hh].append(jnp.sum(s_old * out[:, HEAD_DIM:], axis=0, keepdims=True))
        for hh in range(2):
            qk = jnp.sum(qr[hh] * kr[hh], axis=-1, keepdims=True)
            o = qk * v[hh] + jnp.concatenate(q_s[hh], axis=0) * gam[hh]
            o_ref[:, sls[hh]] = _head_norm_gate(o, gret_ref[:, sls[hh]], g_ref[:, sls[hh]]).astype(o_ref.dtype)


def _ret_decode(proj, states, acc, layer, cos2, sin2, gam_tab, g_ret, bt):
    nb = proj.shape[0]
    col = lambda off: pl.BlockSpec((bt, RET_DIM), lambda i, off=off: (i, off // RET_DIM))
    slab = pl.BlockSpec((None, bt, RET_HEADS, HEAD_DIM, HEAD_DIM), lambda i: (layer, i, 0, 0, 0))
    return pl.pallas_call(
        _ret_decode_kernel,
        out_shape=(jax.ShapeDtypeStruct((nb, RET_DIM), BF16), jax.ShapeDtypeStruct(states.shape, F32)),
        grid=(nb // bt,),
        in_specs=[col(OFF_Q), col(OFF_K), col(OFF_V), col(OFF_GR),
                  pl.BlockSpec((1, HEAD_DIM), lambda i: (0, 0)), pl.BlockSpec((1, HEAD_DIM), lambda i: (0, 0)),
                  pl.BlockSpec((RET_HEADS, SUBLANES, LANES), lambda i: (0, 0, 0)),
                  pl.BlockSpec((1, RET_DIM), lambda i: (0, 0)), slab, pl.BlockSpec(memory_space=pl.ANY)],
        out_specs=(pl.BlockSpec((bt, RET_DIM), lambda i: (i, 0)), slab),
        input_output_aliases={9: 1},
        compiler_params=_params("parallel"),
        name="ret_decode",
    )(proj, proj, proj, proj, cos2, sin2, gam_tab, g_ret, states, acc)


def _sc_prompt_kernel(h_ref, wbg_ref, wcg_ref, wsh_ref, wgs_ref, w_ref, b_ref, o_ref, st_ref, w16_ref, p_even,
                      p_odd):
    @pl.when(pl.program_id(1) == 0)
    def _():
        _load_weight_cols((wbg_ref, wcg_ref, wsh_ref, wgs_ref), w16_ref)

    width = wbg_ref.shape[0]

    def chunk(p, r0, n, tail):
        rows = pl.ds(pl.multiple_of(n * CHUNK, CHUNK), CHUNK)
        part = lambda i: p[r0:r0 + CHUNK, i * width:(i + 1) * width]
        u = part(1) * part(2)
        conv = _causal_conv_rows(tail, u, w_ref, b_ref)
        o_ref[rows, :] = (part(0) * conv * _silu(part(3))).astype(o_ref.dtype)
        return u[CHUNK - SUBLANES:, :]

    tail = _projected_chunks(h_ref, w16_ref, p_even, p_odd, chunk, jnp.zeros((SUBLANES, width), F32))
    st_ref[0] = tail[SUBLANES - (SC_WIDTH - 1):, :]


def _sc_prompt(h, w_t, layer, batch, seq, w, b):
    width = 512
    wrow = lambda off: pl.BlockSpec((None, width, D_MODEL), lambda c, bi, off=off: (layer, off // width + c, 0),
                                    pipeline_mode=pl.Buffered(1))
    return pl.pallas_call(
        _sc_prompt_kernel,
        out_shape=(jax.ShapeDtypeStruct((batch * seq, SC_DIM), BF16),
                   jax.ShapeDtypeStruct((batch, SC_WIDTH - 1, SC_DIM), F32)),
        grid=(SC_DIM // width, batch),
        in_specs=[pl.BlockSpec((seq, D_MODEL), lambda c, bi: (bi, 0)),
                  wrow(OFF_BG), wrow(OFF_CG), wrow(OFF_SH), wrow(OFF_GS),
                  pl.BlockSpec((SC_WIDTH, width), lambda c, bi: (0, c)),
                  pl.BlockSpec((1, width), lambda c, bi: (0, c))],
        out_specs=(pl.BlockSpec((seq, width), lambda c, bi: (bi, c)),
                   pl.BlockSpec((1, SC_WIDTH - 1, width), lambda c, bi: (bi, 0, c))),
        scratch_shapes=[pltpu.VMEM((D_MODEL, 4 * width), BF16),
                        pltpu.VMEM((PROJ_ROWS, 4 * width), F32), pltpu.VMEM((PROJ_ROWS, 4 * width), F32)],
        compiler_params=_params("parallel", "arbitrary"),
        name="sc_prompt",
    )(h, w_t, w_t, w_t, w_t, w, b)


def _sc_decode_kernel(bg_ref, cg_ref, sh_ref, gs_ref, st_ref, w_ref, b_ref, o_ref, stn_ref):
    u = cg_ref[...] * sh_ref[...]
    r0 = st_ref[:, 0:SC_DIM]
    r1 = st_ref[:, SC_DIM:]
    conv = b_ref[...] + r0 * w_ref[0:1, :] + r1 * w_ref[1:2, :] + u * w_ref[2:3, :]
    o_ref[...] = (bg_ref[...] * conv * _silu(gs_ref[...])).astype(o_ref.dtype)
    stn_ref[:, 0:SC_DIM] = r1
    stn_ref[:, SC_DIM:] = u


def _sc_decode(proj, state2d, w, b):
    nb = proj.shape[0]
    col = lambda off: pl.BlockSpec((nb, SC_DIM), lambda i, off=off: (0, off // SC_DIM))
    whole = lambda shape: pl.BlockSpec(shape, lambda i: (0,) * len(shape))
    return pl.pallas_call(
        _sc_decode_kernel,
        out_shape=(jax.ShapeDtypeStruct((nb, SC_DIM), BF16), jax.ShapeDtypeStruct(state2d.shape, F32)),
        grid=(1,),
        in_specs=[col(OFF_BG), col(OFF_CG), col(OFF_SH), col(OFF_GS), whole(state2d.shape),
                  whole((SC_WIDTH, SC_DIM)), whole((1, SC_DIM))],
        out_specs=(whole((nb, SC_DIM)), whole(state2d.shape)),
        compiler_params=_params("arbitrary"),
        name="sc_decode",
    )(proj, proj, proj, proj, state2d, w, b)


def _lane_cumsum(a, lane):
    s = 1
    while s < CHUNK:
        a = a + jnp.where(lane >= s, pltpu.roll(a, s, 1), 0.0)
        s *= 2
    return a


def _ssd_prompt_kernel(h_ref, wz_ref, wx_ref, wb_ref, wc_ref, wdt_ref, dtb_ref, alog_ref, dsk_ref,
                       cwx_ref, cwb_ref, cwc_ref, cbx_ref, cbb_ref, cbc_ref,
                       y_ref, stx_ref, stb_ref, stc_ref, sst_ref, w16_ref, p_even, p_odd, s_scr):
    @pl.when(pl.program_id(1) == 0)
    def _():
        _load_weight_cols((wz_ref, wx_ref, wb_ref, wc_ref, wdt_ref), w16_ref)

    gw = wz_ref.shape[0]
    n_st = wb_ref.shape[0]
    off_x, off_b, off_c, off_dt = gw, 2 * gw, 2 * gw + n_st, 2 * gw + 2 * n_st
    row = lax.broadcasted_iota(jnp.int32, (CHUNK, CHUNK), 0)
    lane = lax.broadcasted_iota(jnp.int32, (CHUNK, CHUNK), 1)
    lane8 = lax.broadcasted_iota(jnp.int32, (HEADS_PER_GROUP, CHUNK), 1)
    causal = row >= lane
    low_lanes = lane < SSM_HEAD_DIM
    low_rows = row < SSM_HEAD_DIM
    a_neg = -jnp.exp(alog_ref[0])
    dt_bias = dtb_ref[0]
    pad_heads = jnp.zeros((CHUNK - HEADS_PER_GROUP, CHUNK), F32)
    s_scr[...] = jnp.zeros_like(s_scr)

    def chunk(p, r0, n, tails):
        tx, tb, tc = tails
        rows = pl.ds(pl.multiple_of(n * CHUNK, CHUNK), CHUNK)
        loc = slice(r0, r0 + CHUNK)
        x_raw = p[loc, off_x:off_x + gw]
        b_raw = p[loc, off_b:off_b + n_st]
        c_raw = p[loc, off_c:off_c + n_st]
        xc = _silu(_causal_conv_rows(tx, x_raw, cwx_ref, cbx_ref))
        bb = _silu(_causal_conv_rows(tb, b_raw, cwb_ref, cbb_ref)).astype(BF16)
        cb16 = _silu(_causal_conv_rows(tc, c_raw, cwc_ref, cbc_ref)).astype(BF16)
        dt_t = _softplus(jnp.transpose(p[loc, off_dt:off_dt + LANES])[0:HEADS_PER_GROUP, :] + dt_bias)
        acum_t = _lane_cumsum(dt_t * a_neg, lane8)
        acum = jnp.transpose(jnp.concatenate([acum_t, pad_heads], axis=0))
        cb = _dot_nt(cb16, bb)
        for pp in range(PAIRS_PER_GROUP):
            sl = slice(pp * PAIR, (pp + 1) * PAIR)
            x_pair = xc[:, sl]
            m, e_col, w_row, dec = [], [], [], []
            for hh in range(2):
                c = 2 * pp + hh
                a_col = jnp.broadcast_to(acum[:, c:c + 1], (CHUNK, CHUNK))
                a_row = acum_t[c:c + 1, :]
                dt_row = dt_t[c:c + 1, :]
                lmat = jnp.exp(jnp.where(causal, a_col - a_row, -jnp.inf))
                m.append((cb * lmat * dt_row).astype(BF16))
                e_col.append(jnp.exp(a_col))
                last = a_col[CHUNK - 1:CHUNK, :]
                w_row.append(jnp.exp(last - a_row) * dt_row)
                dec.append(jnp.exp(last))
            x_lo = jnp.where(low_lanes, x_pair, 0.0).astype(BF16)
            x_hi = jnp.where(low_lanes, 0.0, x_pair).astype(BF16)
            s_old = s_scr[pp]
            y = _dot(m[0], x_lo) + _dot(m[1], x_hi)
            y = y + _dot_nt(cb16, s_old.astype(BF16)) * jnp.where(low_lanes, e_col[0], e_col[1])
            y = y + dsk_ref[0][:, sl] * x_pair
            y_ref[rows, sl] = y * _silu(p[loc, pp * PAIR:(pp + 1) * PAIR])
            xw = (jnp.transpose(x_pair) * jnp.where(low_rows, w_row[0], w_row[1])).astype(BF16)
            s_scr[pp] = jnp.where(low_rows, dec[0], dec[1]) * s_old + _dot(xw, bb)
        cut = CHUNK - SUBLANES
        return x_raw[cut:, :], b_raw[cut:, :], c_raw[cut:, :]

    zeros = lambda width: jnp.zeros((SUBLANES, width), F32)
    tx, tb, tc = _projected_chunks(h_ref, w16_ref, p_even, p_odd, chunk, (zeros(gw), zeros(n_st), zeros(n_st)))
    keep = SUBLANES - (SSM_CONV - 1)
    stx_ref[0] = tx[keep:, :]
    stb_ref[0] = tb[keep:, :]
    stc_ref[0] = tc[keep:, :]
    for pp in range(PAIRS_PER_GROUP):
        for hh in range(2):
            sst_ref[0, 2 * pp + hh] = s_scr[pp, hh * SSM_HEAD_DIM:(hh + 1) * SSM_HEAD_DIM, :]


def _ssd_prompt(h, w_t, layer, batch, seq, dtb, alog, dsk, conv_w, conv_b):
    gw = SSM_DIM // SSM_GROUPS
    n = SSM_STATE
    wrow = lambda off, width: pl.BlockSpec((None, width, D_MODEL),
                                           lambda g, b, off=off, width=width: (layer, off // width + g, 0),
                                           pipeline_mode=pl.Buffered(1))
    grp = lambda rows, width: pl.BlockSpec((1, rows, width), lambda g, b: (g, 0, 0))
    cw = lambda off, width: pl.BlockSpec((SSM_CONV, width), lambda g, b, off=off, width=width: (0, off // width + g))
    cbias = lambda off, width: pl.BlockSpec((1, width), lambda g, b, off=off, width=width: (0, off // width + g))
    st = lambda width: pl.BlockSpec((1, SSM_CONV - 1, width), lambda g, b: (b, 0, g))
    n_proj = 2 * gw + 2 * n + LANES
    return pl.pallas_call(
        _ssd_prompt_kernel,
        out_shape=(jax.ShapeDtypeStruct((batch * seq, SSM_DIM), F32),
                   jax.ShapeDtypeStruct((batch, SSM_CONV - 1, SSM_DIM), F32),
                   jax.ShapeDtypeStruct((batch, SSM_CONV - 1, SSM_GROUPS * n), F32),
                   jax.ShapeDtypeStruct((batch, SSM_CONV - 1, SSM_GROUPS * n), F32),
                   jax.ShapeDtypeStruct((batch, SSM_HEADS, SSM_HEAD_DIM, n), F32)),
        grid=(SSM_GROUPS, batch),
        in_specs=[pl.BlockSpec((seq, D_MODEL), lambda g, b: (b, 0)),
                  wrow(OFF_Z, gw), wrow(OFF_X, gw), wrow(OFF_B, n), wrow(OFF_C, n), wrow(OFF_DT, HEADS_PER_GROUP),
                  grp(HEADS_PER_GROUP, CHUNK), grp(HEADS_PER_GROUP, CHUNK), grp(1, gw),
                  cw(0, gw), cw(SSM_DIM, n), cw(SSM_DIM + SSM_GROUPS * n, n),
                  cbias(0, gw), cbias(SSM_DIM, n), cbias(SSM_DIM + SSM_GROUPS * n, n)],
        out_specs=(pl.BlockSpec((seq, gw), lambda g, b: (b, g)), st(gw), st(n), st(n),
                   pl.BlockSpec((1, HEADS_PER_GROUP, SSM_HEAD_DIM, n), lambda g, b: (b, g, 0, 0))),
        scratch_shapes=[pltpu.VMEM((D_MODEL, n_proj), BF16),
                        pltpu.VMEM((PROJ_ROWS, n_proj), F32), pltpu.VMEM((PROJ_ROWS, n_proj), F32),
                        pltpu.VMEM((PAIRS_PER_GROUP, PAIR, n), F32)],
        compiler_params=_params("parallel", "arbitrary"),
        name="ssd_prompt",
    )(h, w_t, w_t, w_t, w_t, w_t, dtb, alog, dsk, conv_w, conv_w, conv_w, conv_b, conv_b, conv_b)


def _ssd_decode_kernel(z_ref, x_ref, bc_ref, dtr_ref, cst_ref, s_ref, cw_ref, cb_ref, dtb_ref, alog_ref, dsk_ref,
                       acc_ref, y_ref, cstn_ref, sn_ref):
    del acc_ref
    bt = z_ref.shape[0]
    xr = SSM_DIM // LANES
    gr = SSM_GROUPS * SSM_STATE // LANES
    u = jnp.concatenate([x_ref[...], bc_ref[...]], axis=1)
    conv = cb_ref[...]
    for j in range(SSM_CONV - 1):
        conv = conv + cst_ref[:, j] * cw_ref[j]
        cstn_ref[:, j] = u if j == SSM_CONV - 2 else cst_ref[:, j + 1]
    act = _silu(conv + u * cw_ref[SSM_CONV - 1])
    xs = act[:, 0:xr]
    bm = act[:, xr:xr + gr]
    cm = act[:, xr + gr:]
    dt = _softplus(dtr_ref[...] + dtb_ref[...])
    ea = jnp.exp(dt * -jnp.exp(alog_ref[...]))
    xdt = xs * dt
    assert xr == ITEMS == N_PAIRS
    lane = lax.broadcasted_iota(jnp.int32, (LANES, LANES), 1)
    row16 = lax.broadcasted_iota(jnp.int32, (xr, LANES), 0)
    zeros = lambda n: jnp.zeros((n, LANES), F32)
    r_ea = jnp.concatenate([zeros(3 * ITEMS), jnp.ones((3 * ITEMS, LANES), F32), zeros(2 * ITEMS)], axis=0)
    for j in range(bt):
        cbt = zeros(xr)
        b_rows = zeros(xr)
        for g in range(SSM_GROUPS):
            in_group = row16 // PAIRS_PER_GROUP == g
            cb_g = jnp.sum(cm[j, g:g + 1, :] * bm[j, g:g + 1, :], axis=-1, keepdims=True)
            cbt = jnp.where(in_group, cb_g, cbt)
            b_rows = jnp.where(in_group, bm[j, g:g + 1, :], b_rows)
        xd2 = _bf16_parts(xdt[j], 2)
        ea3 = _bf16_parts(ea[j], 3)
        b2 = _bf16_parts(b_rows, 2)
        a = jnp.concatenate([xd2[0], xd2[0], xd2[1]] + ea3 + [zeros(2 * ITEMS)], axis=0)
        cols = jnp.transpose(a)
        r_xb = jnp.concatenate([b2[0], b2[1], b2[0], zeros(5 * ITEMS)], axis=0)
        r = jnp.concatenate([r_xb, r_ea], axis=1).astype(BF16)
        y_t = jnp.zeros((LANES, LANES), F32)
        for pp in range(N_PAIRS):
            g = pp // PAIRS_PER_GROUP
            out = _dot(jnp.where((lane & (ITEMS - 1)) == pp, cols, 0.0).astype(BF16), r)
            s_old = s_ref[j, 2 * pp:2 * pp + 2].reshape(PAIR, SSM_STATE)
            y_col = jnp.sum(s_old * cm[j, g:g + 1, :], axis=-1, keepdims=True)
            y_t = jnp.where(lane == pp, y_col, y_t)
            s_new = out[:, SSM_STATE:] * s_old + out[:, :SSM_STATE]
            sn_ref[j, 2 * pp:2 * pp + 2] = s_new.reshape(2, SSM_HEAD_DIM, SSM_STATE)
        y_inter = jnp.transpose(y_t)[0:xr, :]
        y = cbt * xdt[j] + y_inter * ea[j] + dsk_ref[0] * xs[j]
        y_ref[j] = y * _silu(z_ref[j])


def _ssd_decode(proj3, dt_raw3, cstates, states, acc, layer, conv_w3, conv_b3, dtb3, alog3, dsk3, bt):
    nb = proj3.shape[0]
    xr = SSM_DIM // LANES
    cr = SSM_CONV_DIM // LANES
    whole = lambda shape: pl.BlockSpec(shape, lambda i: (0,) * len(shape))
    cst = pl.BlockSpec((bt, SSM_CONV - 1, cr, LANES), lambda i: (i, 0, 0, 0))
    st_in = pl.BlockSpec((None, bt, SSM_HEADS, SSM_HEAD_DIM, SSM_STATE), lambda i: (layer, i, 0, 0, 0))
    cst_in = pl.BlockSpec((None, bt, SSM_CONV - 1, cr, LANES), lambda i: (layer, i, 0, 0, 0))
    return pl.pallas_call(
        _ssd_decode_kernel,
        out_shape=(jax.ShapeDtypeStruct((nb, xr, LANES), F32), jax.ShapeDtypeStruct(cstates.shape[1:], F32),
                   jax.ShapeDtypeStruct(states.shape, F32)),
        grid=(nb // bt,),
        in_specs=[pl.BlockSpec((bt, xr, LANES), lambda i: (i, OFF_Z // SSM_DIM, 0)),
                  pl.BlockSpec((bt, xr, LANES), lambda i: (i, OFF_X // SSM_DIM, 0)),
                  pl.BlockSpec((bt, SUBLANES, LANES), lambda i: (i, OFF_B // (SUBLANES * LANES), 0)),
                  pl.BlockSpec((bt, xr, LANES), lambda i: (i, 0, 0)),
                  cst_in, st_in, whole((SSM_CONV, cr, LANES)), whole((1, cr, LANES)),
                  whole((1, xr, LANES)), whole((1, xr, LANES)), whole((1, xr, LANES)),
                  pl.BlockSpec(memory_space=pl.ANY)],
        out_specs=(pl.BlockSpec((bt, xr, LANES), lambda i: (i, 0, 0)), cst, st_in),
        input_output_aliases={11: 2},
        compiler_params=_params("parallel"),
        name="ssd_decode",
    )(proj3, proj3, proj3, dt_raw3, cstates, states, conv_w3, conv_b3, dtb3, alog3, dsk3, acc)


def _rope_tables(pos):
    half = HEAD_DIM // 2
    inv = ROPE_BASE ** (-jnp.arange(half, dtype=F32) / half)
    ang = pos.astype(F32)[:, None] * inv[None, :]
    cos, sin = jnp.cos(ang), jnp.sin(ang)
    return jnp.concatenate([cos, cos], axis=-1), jnp.concatenate([-sin, sin], axis=-1)


def _retention_tables():
    lg = jnp.log(1.0 - 2.0 ** (-5.0 - jnp.arange(RET_HEADS, dtype=F32)))
    i = jnp.arange(CHUNK, dtype=F32)
    diff = i[:, None] - i[None, :]
    dmat = jnp.exp(jnp.where((diff >= 0)[None], diff[None] * lg[:, None, None], -jnp.inf))
    full = lambda t: jnp.broadcast_to(t, (RET_HEADS, CHUNK, CHUNK))
    q_dec = full(jnp.exp((i + 1.0)[None, :, None] * lg[:, None, None]))
    k_dec = full(jnp.exp((CHUNK - 1.0 - i)[None, :, None] * lg[:, None, None]))
    c_dec = full(jnp.exp(CHUNK * lg)[:, None, None])
    gam = jnp.broadcast_to(jnp.exp(lg)[:, None, None], (RET_HEADS, SUBLANES, LANES))
    return (dmat, q_dec, k_dec, c_dec), gam


def _per_group(v):
    v = v.astype(F32).reshape(SSM_GROUPS, HEADS_PER_GROUP, 1)
    return jnp.broadcast_to(v, (SSM_GROUPS, HEADS_PER_GROUP, CHUNK))


def _per_lane(v):
    return jnp.repeat(v.astype(F32), SSM_HEAD_DIM)[None, :]


def kernel(x_prompt, x_sample, state_ret, state_sconv, state_ssm_conv, state_ssm, w_in, w_out, norm_pre, norm_post,
           ret_norm, sc_conv_w, sc_conv_b, ssm_conv_w, ssm_conv_b, ssm_dt_bias, ssm_a_log, ssm_d, ssm_norm):
    batch, seq, _ = x_prompt.shape
    nb = x_sample.shape[0]
    depth = w_in.shape[0]
    mp = batch * seq
    xr = SSM_DIM // LANES
    cr = SSM_CONV_DIM // LANES

    cos_p, sin_p = _rope_tables(jnp.arange(seq))
    cos_s, sin_s = _rope_tables(PAST_LEN + jnp.arange(1))
    ret_tabs, gam_tab = _retention_tables()

    w_out16 = w_out.astype(BF16)
    w_t = jnp.swapaxes(w_in, 1, 2)
    w_dt_lane = jnp.repeat(w_t[:, OFF_DT:, :], SSM_HEAD_DIM, axis=1).astype(BF16)

    tm_o = min(256, mp)
    xp = x_prompt.reshape(mp, D_MODEL)
    xs = x_sample.reshape(nb, D_MODEL)
    hp = _prenorm(xp, norm_pre[0][None, :], tm_o)
    hs = _prenorm(xs, norm_pre[0][None, :], nb)

    outs = [[] for _ in range(6)]
    ret_acc = jnp.zeros(state_ret.shape, F32)
    ssm_acc = jnp.zeros(state_ssm.shape, F32)
    for l in range(depth):
        g_next = norm_pre[(l + 1) % depth][None, :]
        dtb_g, alog_g = _per_group(ssm_dt_bias[l]), _per_group(ssm_a_log[l])
        dsk_lane = _per_lane(ssm_d[l])
        dsk_g = dsk_lane.reshape(SSM_GROUPS, 1, SSM_DIM // SSM_GROUPS)
        conv_b = ssm_conv_b[l][None, :]

        o_ret, r_new = _ret_prompt(hp, w_t, l, batch, seq, cos_p, sin_p, ret_tabs, ret_norm[l][None, :])
        o_sc, c_new = _sc_prompt(hp, w_t, l, batch, seq, sc_conv_w[l], sc_conv_b[l][None, :])
        ssm_pre, cx, cb_, cc, s_new = _ssd_prompt(hp, w_t, l, batch, seq, dtb_g, alog_g, dsk_g, ssm_conv_w[l], conv_b)
        xp, hp = _outproj(o_ret, o_sc, ssm_pre, xp, w_out16, l, ssm_norm[l][None, :], norm_post[l][None, :],
                          g_next, tm_o)
        outs[0].append(r_new)
        outs[1].append(c_new)
        outs[2].append(jnp.concatenate([cx, cb_, cc], axis=-1))
        outs[3].append(s_new)

        proj_s = _matmul_nt(hs, w_t, l, N_MAIN, 1024, "inproj_decode")
        dt_s = _matmul_nt(hs, w_dt_lane, l, SSM_DIM, SSM_DIM, "dtproj_decode")
        o_ret, ret_acc = _ret_decode(proj_s, state_ret, ret_acc, l, cos_s, sin_s, gam_tab, ret_norm[l][None, :],
                                     SUBLANES)
        o_sc, c_new = _sc_decode(proj_s, state_sconv[l].reshape(nb, (SC_WIDTH - 1) * SC_DIM), sc_conv_w[l],
                                 sc_conv_b[l][None, :])
        y_s, cs_new, ssm_acc = _ssd_decode(
            proj_s.reshape(nb, N_MAIN // LANES, LANES), dt_s.reshape(nb, xr, LANES),
            state_ssm_conv.reshape(depth, nb, SSM_CONV - 1, cr, LANES), state_ssm, ssm_acc, l,
            ssm_conv_w[l].reshape(SSM_CONV, cr, LANES), conv_b.reshape(1, cr, LANES),
            _per_lane(ssm_dt_bias[l]).reshape(1, xr, LANES), _per_lane(ssm_a_log[l]).reshape(1, xr, LANES),
            dsk_lane.reshape(1, xr, LANES), 4)
        xs, hs = _outproj(o_ret, o_sc, y_s.reshape(nb, SSM_DIM), xs, w_out16, l, ssm_norm[l][None, :],
                          norm_post[l][None, :], g_next, nb)
        outs[4].append(c_new.reshape(nb, SC_WIDTH - 1, SC_DIM))
        outs[5].append(cs_new.reshape(nb, SSM_CONV - 1, SSM_CONV_DIM))

    stacked = [jnp.stack(o) for o in outs]
    return (xp.reshape(batch, seq, D_MODEL), xs.reshape(nb, 1, D_MODEL), *stacked[:4],
            ret_acc, stacked[4], stacked[5], ssm_acc)
```

```python
import functools

import jax
import jax.numpy as jnp
import numpy as np
from jax import lax
from jax.experimental import pallas as pl
from jax.experimental.pallas import tpu as pltpu

F32 = jnp.float32
BF16 = jnp.bfloat16

D_MODEL = 2048
D_MIX = 2 * D_MODEL
RET_HEADS = 8
RET_DIM = 1024
HEAD_DIM = 128
SC_DIM = 1024
SC_WIDTH = 3
SSM_DIM = 2048
SSM_HEAD_DIM = 64
SSM_HEADS = 32
SSM_GROUPS = 4
SSM_STATE = 128
SSM_CONV = 4
SSM_CONV_DIM = SSM_DIM + 2 * SSM_GROUPS * SSM_STATE
CHUNK = 128
PROJ_ROWS = 2 * CHUNK
ROPE_BASE = 10000.0
EPS = 1e-6
PAST_LEN = 16384

OFF_Q, OFF_K, OFF_V, OFF_GR = 0, 1024, 2048, 3072
OFF_BG, OFF_CG, OFF_SH, OFF_GS = 4096, 5120, 6144, 7168
OFF_Z, OFF_X, OFF_B, OFF_C, OFF_DT = 8192, 10240, 12288, 12800, 13312
N_MAIN = OFF_DT

LANES = 128
SUBLANES = 8
V7X_VMEM_LIMIT = 60 * 1024 * 1024
PAIR = 2 * SSM_HEAD_DIM
HEADS_PER_GROUP = SSM_HEADS // SSM_GROUPS
PAIRS_PER_GROUP = HEADS_PER_GROUP // 2
N_PAIRS = SSM_HEADS // 2
ITEMS = 16


def _params(*sem):
    return pltpu.CompilerParams(dimension_semantics=sem, vmem_limit_bytes=V7X_VMEM_LIMIT)


def _silu(x):
    return x * jax.nn.sigmoid(x)


def _softplus(x):
    return jnp.maximum(x, 0.0) + jnp.log1p(jnp.exp(-jnp.abs(x)))


def _dot(a, b):
    return jnp.dot(a, b, preferred_element_type=F32)


def _bf16_parts(x, n):
    parts = []
    for _ in range(n):
        p = x.astype(BF16).astype(F32)
        parts.append(p)
        x = x - p
    return parts


def _dot_nt(a, b):
    return lax.dot_general(a, b, (((1,), (1,)), ((), ())), preferred_element_type=F32)


def _prenorm_kernel(x_ref, g_ref, o_ref):
    x = x_ref[...]
    ms = jnp.mean(x * x, axis=-1, keepdims=True)
    o_ref[...] = (x * lax.rsqrt(ms + EPS) * g_ref[...]).astype(o_ref.dtype)


def _prenorm(x, g, tm):
    m, d = x.shape
    return pl.pallas_call(
        _prenorm_kernel,
        out_shape=jax.ShapeDtypeStruct((m, d), BF16),
        grid=(m // tm,),
        in_specs=[pl.BlockSpec((tm, d), lambda i: (i, 0)), pl.BlockSpec((1, d), lambda i: (0, 0))],
        out_specs=pl.BlockSpec((tm, d), lambda i: (i, 0)),
        compiler_params=_params("parallel"),
        name="prenorm",
    )(x, g)


def _mm_nt_kernel(x_ref, w_ref, o_ref):
    o_ref[...] = _dot_nt(x_ref[...], w_ref[...].astype(BF16))


def _matmul_nt(x, w_t, layer, n_cols, tn, name):
    m, k = x.shape
    return pl.pallas_call(
        _mm_nt_kernel,
        out_shape=jax.ShapeDtypeStruct((m, n_cols), F32),
        grid=(n_cols // tn,),
        in_specs=[pl.BlockSpec((m, k), lambda j: (0, 0)),
                  pl.BlockSpec((None, tn, k), lambda j: (layer, j, 0))],
        out_specs=pl.BlockSpec((m, tn), lambda j: (0, j)),
        compiler_params=_params("parallel"),
        name=name,
    )(x, w_t)


def _outproj_kernel(oret_ref, osc_ref, ssm_ref, x_ref, w_ref, gssm_ref, gpost_ref, gnext_ref, y_ref, hn_ref):
    ypre = ssm_ref[...]
    ms = jnp.mean(ypre * ypre, axis=-1, keepdims=True)
    ossm = (ypre * lax.rsqrt(ms + EPS) * gssm_ref[...]).astype(BF16)
    acc = _dot(oret_ref[...], w_ref[0:RET_DIM, :])
    acc = acc + _dot(osc_ref[...], w_ref[RET_DIM:RET_DIM + SC_DIM, :])
    acc = acc + _dot(ossm, w_ref[RET_DIM + SC_DIM:, :])
    ms2 = jnp.mean(acc * acc, axis=-1, keepdims=True)
    y = x_ref[...] + acc * lax.rsqrt(ms2 + EPS) * gpost_ref[...]
    y_ref[...] = y
    ms3 = jnp.mean(y * y, axis=-1, keepdims=True)
    hn_ref[...] = (y * lax.rsqrt(ms3 + EPS) * gnext_ref[...]).astype(hn_ref.dtype)


def _outproj(oret, osc, ssm_pre, x, w_out, layer, g_ssm, g_post, g_next, tm):
    m = x.shape[0]
    row = lambda width: pl.BlockSpec((tm, width), lambda i: (i, 0))
    vec = lambda width: pl.BlockSpec((1, width), lambda i: (0, 0))
    return pl.pallas_call(
        _outproj_kernel,
        out_shape=(jax.ShapeDtypeStruct((m, D_MODEL), F32), jax.ShapeDtypeStruct((m, D_MODEL), BF16)),
        grid=(m // tm,),
        in_specs=[row(RET_DIM), row(SC_DIM), row(SSM_DIM), row(D_MODEL),
                  pl.BlockSpec((None, D_MIX, D_MODEL), lambda i: (layer, 0, 0), pipeline_mode=pl.Buffered(1)),
                  vec(SSM_DIM), vec(D_MODEL), vec(D_MODEL)],
        out_specs=(row(D_MODEL), row(D_MODEL)),
        compiler_params=_params("parallel"),
        name="outproj",
    )(oret, osc, ssm_pre, x, w_out, g_ssm, g_post, g_next)


def _rope(t, cos2, sin2):
    return t * cos2 + pltpu.roll(t, HEAD_DIM // 2, 1) * sin2


def _head_norm_gate(o, g_row, gate):
    mu = jnp.mean(o, axis=-1, keepdims=True)
    oc = o - mu
    var = jnp.mean(oc * oc, axis=-1, keepdims=True)
    return oc * lax.rsqrt(var + EPS) * g_row * _silu(gate)


def _shift_rows(tail, u, s):
    ext = jnp.concatenate([tail, u], axis=0)
    return ext[SUBLANES - s:SUBLANES - s + u.shape[0]]


def _causal_conv_rows(tail, u, w_ref, b_ref):
    width = w_ref.shape[0]
    out = b_ref[...]
    for j in range(width - 1):
        out = out + _shift_rows(tail, u, width - 1 - j) * w_ref[j:j + 1, :]
    return out + u * w_ref[width - 1:width, :]


def _load_weight_cols(w_refs, w16_ref):
    off = 0
    for w_ref in w_refs:
        n = w_ref.shape[0]
        for r in range(0, n, LANES):
            blk = w_ref[r:min(r + LANES, n), :]
            if blk.shape[0] < LANES:
                blk = jnp.concatenate([blk, jnp.zeros((LANES - blk.shape[0], blk.shape[1]), F32)], axis=0)
            w16_ref[:, off:off + LANES] = jnp.transpose(blk).astype(BF16)
            off += LANES


def _projected_chunks(h_ref, w16_ref, p_even, p_odd, chunk_fn, carry):
    proj_rows = p_even.shape[0]
    assert h_ref.shape[0] % proj_rows == 0 and proj_rows % CHUNK == 0
    steps = h_ref.shape[0] // proj_rows
    per_step = proj_rows // CHUNK

    def project(step, p_ref):
        rows = pl.ds(pl.multiple_of(step * proj_rows, proj_rows), proj_rows)
        p_ref[...] = _dot(h_ref[rows, :], w16_ref[...])

    def mix(step, p_ref, carry):
        for c in range(per_step):
            carry = chunk_fn(p_ref, c * CHUNK, step * per_step + c, carry)
        return carry

    def body(i, carry):
        project(2 * i + 1, p_odd)
        carry = mix(2 * i, p_even, carry)
        project(2 * i + 2, p_even)
        return mix(2 * i + 1, p_odd, carry)

    project(0, p_even)
    full = (steps - 1) // 2
    carry = lax.fori_loop(0, full, body, carry)
    if steps - 2 * full == 2:
        project(2 * full + 1, p_odd)
        carry = mix(2 * full, p_even, carry)
        return mix(2 * full + 1, p_odd, carry)
    return mix(2 * full, p_even, carry)


def _ret_prompt_kernel(h_ref, wq_ref, wk_ref, wv_ref, wg_ref, cos_ref, sin_ref, dmat_ref, qdec_ref, kdec_ref,
                       cdec_ref, gret_ref, o_ref, st_ref, w16_ref, p_even, p_odd, s_scr):
    @pl.when(pl.program_id(1) == 0)
    def _():
        _load_weight_cols((wq_ref, wk_ref, wv_ref, wg_ref), w16_ref)

    width = wq_ref.shape[0]
    heads = width // HEAD_DIM
    s_scr[...] = jnp.zeros_like(s_scr)

    def chunk(p, r0, n, carry):
        rows = pl.ds(pl.multiple_of(n * CHUNK, CHUNK), CHUNK)
        loc = slice(r0, r0 + CHUNK)
        cos2 = cos_ref[rows, :]
        sin2 = sin_ref[rows, :]
        for hh in range(heads):
            sl = slice(hh * HEAD_DIM, (hh + 1) * HEAD_DIM)
            part = lambda i: p[loc, i * width + hh * HEAD_DIM:i * width + (hh + 1) * HEAD_DIM]
            qr = _rope(part(0), cos2, sin2)
            kr = _rope(part(1), cos2, sin2) * (HEAD_DIM ** -0.5)
            qb = qr.astype(BF16)
            vb = part(2).astype(BF16)
            scores = _dot_nt(qb, kr.astype(BF16)) * dmat_ref[hh]
            s_old = s_scr[hh]
            lhs = jnp.concatenate([scores.astype(BF16), (qr * qdec_ref[hh]).astype(BF16)], axis=1)
            o = _dot(lhs, jnp.concatenate([vb, s_old.astype(BF16)], axis=0))
            kd_t = jnp.transpose(kr * kdec_ref[hh]).astype(BF16)
            s_scr[hh] = cdec_ref[hh] * s_old + _dot(kd_t, vb)
            o_ref[rows, sl] = _head_norm_gate(o, gret_ref[:, sl], part(3)).astype(o_ref.dtype)
        return carry

    _projected_chunks(h_ref, w16_ref, p_even, p_odd, chunk, 0)
    st_ref[0] = s_scr[...]


def _ret_prompt(h, w_t, layer, batch, seq, cos2, sin2, tabs, g_ret):
    hp = 4
    width = hp * HEAD_DIM
    wrow = lambda off: pl.BlockSpec((None, width, D_MODEL), lambda c, b, off=off: (layer, off // width + c, 0),
                                    pipeline_mode=pl.Buffered(1))
    tab = pl.BlockSpec((hp, CHUNK, CHUNK), lambda c, b: (c, 0, 0), pipeline_mode=pl.Buffered(1))
    full = pl.BlockSpec((seq, HEAD_DIM), lambda c, b: (0, 0), pipeline_mode=pl.Buffered(1))
    return pl.pallas_call(
        _ret_prompt_kernel,
        out_shape=(jax.ShapeDtypeStruct((batch * seq, RET_DIM), BF16),
                   jax.ShapeDtypeStruct((batch, RET_HEADS, HEAD_DIM, HEAD_DIM), F32)),
        grid=(RET_HEADS // hp, batch),
        in_specs=[pl.BlockSpec((seq, D_MODEL), lambda c, b: (b, 0)),
                  wrow(OFF_Q), wrow(OFF_K), wrow(OFF_V), wrow(OFF_GR), full, full, tab, tab, tab, tab,
                  pl.BlockSpec((1, width), lambda c, b: (0, c))],
        out_specs=(pl.BlockSpec((seq, width), lambda c, b: (b, c)),
                   pl.BlockSpec((1, hp, HEAD_DIM, HEAD_DIM), lambda c, b: (b, c, 0, 0))),
        scratch_shapes=[pltpu.VMEM((D_MODEL, 4 * width), BF16),
                        pltpu.VMEM((PROJ_ROWS, 4 * width), F32), pltpu.VMEM((PROJ_ROWS, 4 * width), F32),
                        pltpu.VMEM((hp, HEAD_DIM, HEAD_DIM), F32)],
        compiler_params=_params("parallel", "arbitrary"),
        name="ret_prompt",
    )(h, w_t, w_t, w_t, w_t, cos2, sin2, *tabs, g_ret)


def _ret_decode_kernel(q_ref, k_ref, v_ref, g_ref, cos_ref, sin_ref, gam_ref, gret_ref, s_ref, acc_ref, o_ref,
                       sn_ref):
    del acc_ref
    bt = q_ref.shape[0]
    assert 2 * bt == ITEMS
    cos2 = cos_ref[...]
    sin2 = sin_ref[...]
    lane = lax.broadcasted_iota(jnp.int32, (LANES, LANES), 1)
    zero = jnp.zeros((bt, HEAD_DIM), F32)
    one = jnp.ones((bt, HEAD_DIM), F32)
    for hp in range(RET_HEADS // 2):
        qr, kr, v, gam, sls = [], [], [], [], []
        for hh in range(2):
            h = 2 * hp + hh
            sl = slice(h * HEAD_DIM, (h + 1) * HEAD_DIM)
            sls.append(sl)
            gam.append(gam_ref[h][0:1, :])
            qr.append(_rope(q_ref[:, sl], cos2, sin2))
            kr.append(_rope(k_ref[:, sl], cos2, sin2) * (HEAD_DIM ** -0.5))
            v.append(v_ref[:, sl])
        k2 = [_bf16_parts(t, 2) for t in kr]
        q3 = [_bf16_parts(t, 3) for t in qr]
        v2 = [_bf16_parts(t, 2) for t in v]
        both = lambda parts, n: [parts[0][n], parts[1][n]]
        a = jnp.concatenate(both(k2, 0) + both(k2, 0) + both(k2, 1) + both(q3, 0) + both(q3, 1) + both(q3, 2)
                            + [zero] * 4, axis=0)
        cols = jnp.transpose(a)
        r_kv = jnp.concatenate(both(v2, 0) + both(v2, 1) + both(v2, 0) + [zero] * 10, axis=0)
        r_q = jnp.concatenate([zero] * 6 + [one] * 6 + [zero] * 4, axis=0)
        r = jnp.concatenate([r_kv, r_q], axis=1).astype(BF16)
        q_s = [[], []]
        for i in range(ITEMS):
            hh, j = divmod(i, bt)
            h = 2 * hp + hh
            out = _dot(jnp.where((lane & (ITEMS - 1)) == i, cols, 0.0).astype(BF16), r)
            s_old = s_ref[j, h]
            sn_ref[j, h] = gam[hh] * s_old + out[:, :HEAD_DIM]
            q_s[hh].append(jnp.sum(s_old * out[:, HEAD_DIM:], axis=0, keepdims=True))
        for hh in range(2):
            qk = jnp.sum(qr[hh] * kr[hh], axis=-1, keepdims=True)
            o = qk * v[hh] + jnp.concatenate(q_s[hh], axis=0) * gam[hh]
            o_ref[:, sls[hh]] = _head_norm_gate(o, gret_ref[:, sls[hh]], g_ref[:, sls[hh]]).astype(o_ref.dtype)


def _ret_decode(proj, states, acc, layer, cos2, sin2, gam_tab, g_ret, bt):
    nb = proj.shape[0]
    col = lambda off: pl.BlockSpec((bt, RET_DIM), lambda i, off=off: (i, off // RET_DIM))
    slab = pl.BlockSpec((None, bt, RET_HEADS, HEAD_DIM, HEAD_DIM), lambda i: (layer, i, 0, 0, 0))
    return pl.pallas_call(
        _ret_decode_kernel,
        out_shape=(jax.ShapeDtypeStruct((nb, RET_DIM), BF16), jax.ShapeDtypeStruct(states.shape, F32)),
        grid=(nb // bt,),
        in_specs=[col(OFF_Q), col(OFF_K), col(OFF_V), col(OFF_GR),
                  pl.BlockSpec((1, HEAD_DIM), lambda i: (0, 0)), pl.BlockSpec((1, HEAD_DIM), lambda i: (0, 0)),
                  pl.BlockSpec((RET_HEADS, SUBLANES, LANES), lambda i: (0, 0, 0)),
                  pl.BlockSpec((1, RET_DIM), lambda i: (0, 0)), slab, pl.BlockSpec(memory_space=pl.ANY)],
        out_specs=(pl.BlockSpec((bt, RET_DIM), lambda i: (i, 0)), slab),
        input_output_aliases={9: 1},
        compiler_params=_params("parallel"),
        name="ret_decode",
    )(proj, proj, proj, proj, cos2, sin2, gam_tab, g_ret, states, acc)


def _sc_prompt_kernel(h_ref, wbg_ref, wcg_ref, wsh_ref, wgs_ref, w_ref, b_ref, o_ref, st_ref, w16_ref, p_even,
                      p_odd):
    @pl.when(pl.program_id(1) == 0)
    def _():
        _load_weight_cols((wbg_ref, wcg_ref, wsh_ref, wgs_ref), w16_ref)

    width = wbg_ref.shape[0]

    def chunk(p, r0, n, tail):
        rows = pl.ds(pl.multiple_of(n * CHUNK, CHUNK), CHUNK)
        part = lambda i: p[r0:r0 + CHUNK, i * width:(i + 1) * width]
        u = part(1) * part(2)
        conv = _causal_conv_rows(tail, u, w_ref, b_ref)
        o_ref[rows, :] = (part(0) * conv * _silu(part(3))).astype(o_ref.dtype)
        return u[CHUNK - SUBLANES:, :]

    tail = _projected_chunks(h_ref, w16_ref, p_even, p_odd, chunk, jnp.zeros((SUBLANES, width), F32))
    st_ref[0] = tail[SUBLANES - (SC_WIDTH - 1):, :]


def _sc_prompt(h, w_t, layer, batch, seq, w, b):
    width = 512
    proj_rows = min(2 * PROJ_ROWS, seq)
    wrow = lambda off: pl.BlockSpec((None, width, D_MODEL), lambda c, bi, off=off: (layer, off // width + c, 0),
                                    pipeline_mode=pl.Buffered(1))
    return pl.pallas_call(
        _sc_prompt_kernel,
        out_shape=(jax.ShapeDtypeStruct((batch * seq, SC_DIM), BF16),
                   jax.ShapeDtypeStruct((batch, SC_WIDTH - 1, SC_DIM), F32)),
        grid=(SC_DIM // width, batch),
        in_specs=[pl.BlockSpec((seq, D_MODEL), lambda c, bi: (bi, 0)),
                  wrow(OFF_BG), wrow(OFF_CG), wrow(OFF_SH), wrow(OFF_GS),
                  pl.BlockSpec((SC_WIDTH, width), lambda c, bi: (0, c)),
                  pl.BlockSpec((1, width), lambda c, bi: (0, c))],
        out_specs=(pl.BlockSpec((seq, width), lambda c, bi: (bi, c)),
                   pl.BlockSpec((1, SC_WIDTH - 1, width), lambda c, bi: (bi, 0, c))),
        scratch_shapes=[pltpu.VMEM((D_MODEL, 4 * width), BF16),
                        pltpu.VMEM((proj_rows, 4 * width), F32), pltpu.VMEM((proj_rows, 4 * width), F32)],
        compiler_params=_params("parallel", "arbitrary"),
        name="sc_prompt",
    )(h, w_t, w_t, w_t, w_t, w, b)


def _sc_decode_kernel(bg_ref, cg_ref, sh_ref, gs_ref, st_ref, w_ref, b_ref, o_ref, stn_ref):
    u = cg_ref[...] * sh_ref[...]
    r0 = st_ref[:, 0:SC_DIM]
    r1 = st_ref[:, SC_DIM:]
    conv = b_ref[...] + r0 * w_ref[0:1, :] + r1 * w_ref[1:2, :] + u * w_ref[2:3, :]
    o_ref[...] = (bg_ref[...] * conv * _silu(gs_ref[...])).astype(o_ref.dtype)
    stn_ref[:, 0:SC_DIM] = r1
    stn_ref[:, SC_DIM:] = u


def _sc_decode(proj, state2d, w, b):
    nb = proj.shape[0]
    col = lambda off: pl.BlockSpec((nb, SC_DIM), lambda i, off=off: (0, off // SC_DIM))
    whole = lambda shape: pl.BlockSpec(shape, lambda i: (0,) * len(shape))
    return pl.pallas_call(
        _sc_decode_kernel,
        out_shape=(jax.ShapeDtypeStruct((nb, SC_DIM), BF16), jax.ShapeDtypeStruct(state2d.shape, F32)),
        grid=(1,),
        in_specs=[col(OFF_BG), col(OFF_CG), col(OFF_SH), col(OFF_GS), whole(state2d.shape),
                  whole((SC_WIDTH, SC_DIM)), whole((1, SC_DIM))],
        out_specs=(whole((nb, SC_DIM)), whole(state2d.shape)),
        compiler_params=_params("arbitrary"),
        name="sc_decode",
    )(proj, proj, proj, proj, state2d, w, b)


def _lane_cumsum(a, lane):
    s = 1
    while s < CHUNK:
        a = a + jnp.where(lane >= s, pltpu.roll(a, s, 1), 0.0)
        s *= 2
    return a


def _ssd_prompt_kernel(h_ref, wz_ref, wx_ref, wb_ref, wc_ref, wdt_ref, dtb_ref, alog_ref, dsk_ref,
                       cwx_ref, cwb_ref, cwc_ref, cbx_ref, cbb_ref, cbc_ref,
                       y_ref, stx_ref, stb_ref, stc_ref, sst_ref, w16_ref, p_even, p_odd, s_scr):
    @pl.when(pl.program_id(1) == 0)
    def _():
        _load_weight_cols((wz_ref, wx_ref, wb_ref, wc_ref, wdt_ref), w16_ref)

    gw = wz_ref.shape[0]
    n_st = wb_ref.shape[0]
    off_x, off_b, off_c, off_dt = gw, 2 * gw, 2 * gw + n_st, 2 * gw + 2 * n_st
    row = lax.broadcasted_iota(jnp.int32, (CHUNK, CHUNK), 0)
    lane = lax.broadcasted_iota(jnp.int32, (CHUNK, CHUNK), 1)
    lane8 = lax.broadcasted_iota(jnp.int32, (HEADS_PER_GROUP, CHUNK), 1)
    causal = row >= lane
    low_lanes = lane < SSM_HEAD_DIM
    low_rows = row < SSM_HEAD_DIM
    a_neg = -jnp.exp(alog_ref[0])
    dt_bias = dtb_ref[0]
    pad_heads = jnp.zeros((CHUNK - HEADS_PER_GROUP, CHUNK), F32)
    s_scr[...] = jnp.zeros_like(s_scr)

    def chunk(p, r0, n, tails):
        tx, tb, tc = tails
        rows = pl.ds(pl.multiple_of(n * CHUNK, CHUNK), CHUNK)
        loc = slice(r0, r0 + CHUNK)
        x_raw = p[loc, off_x:off_x + gw]
        b_raw = p[loc, off_b:off_b + n_st]
        c_raw = p[loc, off_c:off_c + n_st]
        xc = _silu(_causal_conv_rows(tx, x_raw, cwx_ref, cbx_ref))
        bb = _silu(_causal_conv_rows(tb, b_raw, cwb_ref, cbb_ref)).astype(BF16)
        cb16 = _silu(_causal_conv_rows(tc, c_raw, cwc_ref, cbc_ref)).astype(BF16)
        dt_t = _softplus(jnp.transpose(p[loc, off_dt:off_dt + LANES])[0:HEADS_PER_GROUP, :] + dt_bias)
        acum_t = _lane_cumsum(dt_t * a_neg, lane8)
        acum = jnp.transpose(jnp.concatenate([acum_t, pad_heads], axis=0))
        cb = _dot_nt(cb16, bb)
        y_intra, e_sel, dec_sel, xw, s_old = [], [], [], [], []
        for pp in range(PAIRS_PER_GROUP):
            x_pair = xc[:, pp * PAIR:(pp + 1) * PAIR]
            m, e_col, w_row, dec = [], [], [], []
            for hh in range(2):
                c = 2 * pp + hh
                a_col = jnp.broadcast_to(acum[:, c:c + 1], (CHUNK, CHUNK))
                a_row = acum_t[c:c + 1, :]
                dt_row = dt_t[c:c + 1, :]
                lmat = jnp.exp(jnp.where(causal, a_col - a_row, -jnp.inf))
                m.append((cb * lmat * dt_row).astype(BF16))
                e_col.append(jnp.exp(a_col))
                last = a_col[CHUNK - 1:CHUNK, :]
                w_row.append(jnp.exp(last - a_row) * dt_row)
                dec.append(jnp.exp(last))
            x_stack = jnp.concatenate([jnp.where(low_lanes, x_pair, 0.0), jnp.where(low_lanes, 0.0, x_pair)], axis=0)
            y_intra.append(_dot(jnp.concatenate(m, axis=1), x_stack.astype(BF16)))
            e_sel.append(jnp.where(low_lanes, e_col[0], e_col[1]))
            dec_sel.append(jnp.where(low_rows, dec[0], dec[1]))
            xw.append((jnp.transpose(x_pair) * jnp.where(low_rows, w_row[0], w_row[1])).astype(BF16))
            s_old.append(s_scr[pp])
        s_upd = _dot(jnp.concatenate(xw, axis=0), bb)
        for q in range(PAIRS_PER_GROUP // 2):
            s_two = jnp.concatenate([s_old[2 * q], s_old[2 * q + 1]], axis=0).astype(BF16)
            y_two = _dot_nt(cb16, s_two)
            for r in range(2):
                pp = 2 * q + r
                sl = slice(pp * PAIR, (pp + 1) * PAIR)
                y = y_intra[pp] + y_two[:, r * PAIR:(r + 1) * PAIR] * e_sel[pp] + dsk_ref[0][:, sl] * xc[:, sl]
                y_ref[rows, sl] = y * _silu(p[loc, sl])
                s_scr[pp] = dec_sel[pp] * s_old[pp] + s_upd[pp * PAIR:(pp + 1) * PAIR, :]
        cut = CHUNK - SUBLANES
        return x_raw[cut:, :], b_raw[cut:, :], c_raw[cut:, :]

    zeros = lambda width: jnp.zeros((SUBLANES, width), F32)
    tx, tb, tc = _projected_chunks(h_ref, w16_ref, p_even, p_odd, chunk, (zeros(gw), zeros(n_st), zeros(n_st)))
    keep = SUBLANES - (SSM_CONV - 1)
    stx_ref[0] = tx[keep:, :]
    stb_ref[0] = tb[keep:, :]
    stc_ref[0] = tc[keep:, :]
    for pp in range(PAIRS_PER_GROUP):
        for hh in range(2):
            sst_ref[0, 2 * pp + hh] = s_scr[pp, hh * SSM_HEAD_DIM:(hh + 1) * SSM_HEAD_DIM, :]


def _ssd_prompt(h, w_t, layer, batch, seq, dtb, alog, dsk, conv_w, conv_b):
    gw = SSM_DIM // SSM_GROUPS
    n = SSM_STATE
    wrow = lambda off, width: pl.BlockSpec((None, width, D_MODEL),
                                           lambda g, b, off=off, width=width: (layer, off // width + g, 0),
                                           pipeline_mode=pl.Buffered(1))
    grp = lambda rows, width: pl.BlockSpec((1, rows, width), lambda g, b: (g, 0, 0))
    cw = lambda off, width: pl.BlockSpec((SSM_CONV, width), lambda g, b, off=off, width=width: (0, off // width + g))
    cbias = lambda off, width: pl.BlockSpec((1, width), lambda g, b, off=off, width=width: (0, off // width + g))
    st = lambda width: pl.BlockSpec((1, SSM_CONV - 1, width), lambda g, b: (b, 0, g))
    n_proj = 2 * gw + 2 * n + LANES
    return pl.pallas_call(
        _ssd_prompt_kernel,
        out_shape=(jax.ShapeDtypeStruct((batch * seq, SSM_DIM), F32),
                   jax.ShapeDtypeStruct((batch, SSM_CONV - 1, SSM_DIM), F32),
                   jax.ShapeDtypeStruct((batch, SSM_CONV - 1, SSM_GROUPS * n), F32),
                   jax.ShapeDtypeStruct((batch, SSM_CONV - 1, SSM_GROUPS * n), F32),
                   jax.ShapeDtypeStruct((batch, SSM_HEADS, SSM_HEAD_DIM, n), F32)),
        grid=(SSM_GROUPS, batch),
        in_specs=[pl.BlockSpec((seq, D_MODEL), lambda g, b: (b, 0)),
                  wrow(OFF_Z, gw), wrow(OFF_X, gw), wrow(OFF_B, n), wrow(OFF_C, n), wrow(OFF_DT, HEADS_PER_GROUP),
                  grp(HEADS_PER_GROUP, CHUNK), grp(HEADS_PER_GROUP, CHUNK), grp(1, gw),
                  cw(0, gw), cw(SSM_DIM, n), cw(SSM_DIM + SSM_GROUPS * n, n),
                  cbias(0, gw), cbias(SSM_DIM, n), cbias(SSM_DIM + SSM_GROUPS * n, n)],
        out_specs=(pl.BlockSpec((seq, gw), lambda g, b: (b, g)), st(gw), st(n), st(n),
                   pl.BlockSpec((1, HEADS_PER_GROUP, SSM_HEAD_DIM, n), lambda g, b: (b, g, 0, 0))),
        scratch_shapes=[pltpu.VMEM((D_MODEL, n_proj), BF16),
                        pltpu.VMEM((PROJ_ROWS, n_proj), F32), pltpu.VMEM((PROJ_ROWS, n_proj), F32),
                        pltpu.VMEM((PAIRS_PER_GROUP, PAIR, n), F32)],
        compiler_params=_params("parallel", "arbitrary"),
        name="ssd_prompt",
    )(h, w_t, w_t, w_t, w_t, w_t, dtb, alog, dsk, conv_w, conv_w, conv_w, conv_b, conv_b, conv_b)


def _ssd_decode_kernel(z_ref, x_ref, bc_ref, dtr_ref, cst_ref, s_ref, cw_ref, cb_ref, dtb_ref, alog_ref, dsk_ref,
                       acc_ref, y_ref, cstn_ref, sn_ref):
    del acc_ref
    bt = z_ref.shape[0]
    xr = SSM_DIM // LANES
    gr = SSM_GROUPS * SSM_STATE // LANES
    u = jnp.concatenate([x_ref[...], bc_ref[...]], axis=1)
    conv = cb_ref[...]
    for j in range(SSM_CONV - 1):
        conv = conv + cst_ref[:, j] * cw_ref[j]
        cstn_ref[:, j] = u if j == SSM_CONV - 2 else cst_ref[:, j + 1]
    act = _silu(conv + u * cw_ref[SSM_CONV - 1])
    xs = act[:, 0:xr]
    bm = act[:, xr:xr + gr]
    cm = act[:, xr + gr:]
    dt = _softplus(dtr_ref[...] + dtb_ref[...])
    ea = jnp.exp(dt * -jnp.exp(alog_ref[...]))
    xdt = xs * dt
    assert xr == ITEMS == N_PAIRS
    lane = lax.broadcasted_iota(jnp.int32, (LANES, LANES), 1)
    row16 = lax.broadcasted_iota(jnp.int32, (xr, LANES), 0)
    zeros = lambda n: jnp.zeros((n, LANES), F32)
    r_ea = jnp.concatenate([zeros(3 * ITEMS), jnp.ones((3 * ITEMS, LANES), F32), zeros(2 * ITEMS)], axis=0)
    for j in range(bt):
        cbt = zeros(xr)
        b_rows = zeros(xr)
        for g in range(SSM_GROUPS):
            in_group = row16 // PAIRS_PER_GROUP == g
            cb_g = jnp.sum(cm[j, g:g + 1, :] * bm[j, g:g + 1, :], axis=-1, keepdims=True)
            cbt = jnp.where(in_group, cb_g, cbt)
            b_rows = jnp.where(in_group, bm[j, g:g + 1, :], b_rows)
        xd2 = _bf16_parts(xdt[j], 2)
        ea3 = _bf16_parts(ea[j], 3)
        b2 = _bf16_parts(b_rows, 2)
        a = jnp.concatenate([xd2[0], xd2[0], xd2[1]] + ea3 + [zeros(2 * ITEMS)], axis=0)
        cols = jnp.transpose(a)
        r_xb = jnp.concatenate([b2[0], b2[1], b2[0], zeros(5 * ITEMS)], axis=0)
        r = jnp.concatenate([r_xb, r_ea], axis=1).astype(BF16)
        y_t = jnp.zeros((LANES, LANES), F32)
        for pp in range(N_PAIRS):
            g = pp // PAIRS_PER_GROUP
            out = _dot(jnp.where((lane & (ITEMS - 1)) == pp, cols, 0.0).astype(BF16), r)
            s_old = s_ref[j, 2 * pp:2 * pp + 2].reshape(PAIR, SSM_STATE)
            y_col = jnp.sum(s_old * cm[j, g:g + 1, :], axis=-1, keepdims=True)
            y_t = jnp.where(lane == pp, y_col, y_t)
            s_new = out[:, SSM_STATE:] * s_old + out[:, :SSM_STATE]
            sn_ref[j, 2 * pp:2 * pp + 2] = s_new.reshape(2, SSM_HEAD_DIM, SSM_STATE)
        y_inter = jnp.transpose(y_t)[0:xr, :]
        y = cbt * xdt[j] + y_inter * ea[j] + dsk_ref[0] * xs[j]
        y_ref[j] = y * _silu(z_ref[j])


def _ssd_decode(proj3, dt_raw3, cstates, states, acc, layer, conv_w3, conv_b3, dtb3, alog3, dsk3, bt):
    nb = proj3.shape[0]
    xr = SSM_DIM // LANES
    cr = SSM_CONV_DIM // LANES
    whole = lambda shape: pl.BlockSpec(shape, lambda i: (0,) * len(shape))
    cst = pl.BlockSpec((bt, SSM_CONV - 1, cr, LANES), lambda i: (i, 0, 0, 0))
    st_in = pl.BlockSpec((None, bt, SSM_HEADS, SSM_HEAD_DIM, SSM_STATE), lambda i: (layer, i, 0, 0, 0))
    cst_in = pl.BlockSpec((None, bt, SSM_CONV - 1, cr, LANES), lambda i: (layer, i, 0, 0, 0))
    return pl.pallas_call(
        _ssd_decode_kernel,
        out_shape=(jax.ShapeDtypeStruct((nb, xr, LANES), F32), jax.ShapeDtypeStruct(cstates.shape[1:], F32),
                   jax.ShapeDtypeStruct(states.shape, F32)),
        grid=(nb // bt,),
        in_specs=[pl.BlockSpec((bt, xr, LANES), lambda i: (i, OFF_Z // SSM_DIM, 0)),
                  pl.BlockSpec((bt, xr, LANES), lambda i: (i, OFF_X // SSM_DIM, 0)),
                  pl.BlockSpec((bt, SUBLANES, LANES), lambda i: (i, OFF_B // (SUBLANES * LANES), 0)),
                  pl.BlockSpec((bt, xr, LANES), lambda i: (i, 0, 0)),
                  cst_in, st_in, whole((SSM_CONV, cr, LANES)), whole((1, cr, LANES)),
                  whole((1, xr, LANES)), whole((1, xr, LANES)), whole((1, xr, LANES)),
                  pl.BlockSpec(memory_space=pl.ANY)],
        out_specs=(pl.BlockSpec((bt, xr, LANES), lambda i: (i, 0, 0)), cst, st_in),
        input_output_aliases={11: 2},
        compiler_params=_params("parallel"),
        name="ssd_decode",
    )(proj3, proj3, proj3, dt_raw3, cstates, states, conv_w3, conv_b3, dtb3, alog3, dsk3, acc)


def _rope_tables(pos):
    half = HEAD_DIM // 2
    inv = ROPE_BASE ** (-jnp.arange(half, dtype=F32) / half)
    ang = pos.astype(F32)[:, None] * inv[None, :]
    cos, sin = jnp.cos(ang), jnp.sin(ang)
    return jnp.concatenate([cos, cos], axis=-1), jnp.concatenate([-sin, sin], axis=-1)


def _retention_tables():
    lg = jnp.log(1.0 - 2.0 ** (-5.0 - jnp.arange(RET_HEADS, dtype=F32)))
    i = jnp.arange(CHUNK, dtype=F32)
    diff = i[:, None] - i[None, :]
    dmat = jnp.exp(jnp.where((diff >= 0)[None], diff[None] * lg[:, None, None], -jnp.inf))
    full = lambda t: jnp.broadcast_to(t, (RET_HEADS, CHUNK, CHUNK))
    q_dec = full(jnp.exp((i + 1.0)[None, :, None] * lg[:, None, None]))
    k_dec = full(jnp.exp((CHUNK - 1.0 - i)[None, :, None] * lg[:, None, None]))
    c_dec = full(jnp.exp(CHUNK * lg)[:, None, None])
    gam = jnp.broadcast_to(jnp.exp(lg)[:, None, None], (RET_HEADS, SUBLANES, LANES))
    return (dmat, q_dec, k_dec, c_dec), gam


def _per_group(v):
    v = v.astype(F32).reshape(SSM_GROUPS, HEADS_PER_GROUP, 1)
    return jnp.broadcast_to(v, (SSM_GROUPS, HEADS_PER_GROUP, CHUNK))


def _per_lane(v):
    return jnp.repeat(v.astype(F32), SSM_HEAD_DIM)[None, :]


def kernel(x_prompt, x_sample, state_ret, state_sconv, state_ssm_conv, state_ssm, w_in, w_out, norm_pre, norm_post,
           ret_norm, sc_conv_w, sc_conv_b, ssm_conv_w, ssm_conv_b, ssm_dt_bias, ssm_a_log, ssm_d, ssm_norm):
    batch, seq, _ = x_prompt.shape
    nb = x_sample.shape[0]
    depth = w_in.shape[0]
    mp = batch * seq
    xr = SSM_DIM // LANES
    cr = SSM_CONV_DIM // LANES

    cos_p, sin_p = _rope_tables(jnp.arange(seq))
    cos_s, sin_s = _rope_tables(PAST_LEN + jnp.arange(1))
    ret_tabs, gam_tab = _retention_tables()

    w_out16 = w_out.astype(BF16)
    w_t = jnp.swapaxes(w_in, 1, 2)
    w_dt_lane = jnp.repeat(w_t[:, OFF_DT:, :], SSM_HEAD_DIM, axis=1).astype(BF16)

    tm_o = min(256, mp)
    xp = x_prompt.reshape(mp, D_MODEL)
    xs = x_sample.reshape(nb, D_MODEL)
    hp = _prenorm(xp, norm_pre[0][None, :], tm_o)
    hs = _prenorm(xs, norm_pre[0][None, :], nb)

    outs = [[] for _ in range(6)]
    ret_acc = jnp.zeros(state_ret.shape, F32)
    ssm_acc = jnp.zeros(state_ssm.shape, F32)
    for l in range(depth):
        g_next = norm_pre[(l + 1) % depth][None, :]
        dtb_g, alog_g = _per_group(ssm_dt_bias[l]), _per_group(ssm_a_log[l])
        dsk_lane = _per_lane(ssm_d[l])
        dsk_g = dsk_lane.reshape(SSM_GROUPS, 1, SSM_DIM // SSM_GROUPS)
        conv_b = ssm_conv_b[l][None, :]

        o_ret, r_new = _ret_prompt(hp, w_t, l, batch, seq, cos_p, sin_p, ret_tabs, ret_norm[l][None, :])
        o_sc, c_new = _sc_prompt(hp, w_t, l, batch, seq, sc_conv_w[l], sc_conv_b[l][None, :])
        ssm_pre, cx, cb_, cc, s_new = _ssd_prompt(hp, w_t, l, batch, seq, dtb_g, alog_g, dsk_g, ssm_conv_w[l], conv_b)
        xp, hp = _outproj(o_ret, o_sc, ssm_pre, xp, w_out16, l, ssm_norm[l][None, :], norm_post[l][None, :],
                          g_next, tm_o)
        outs[0].append(r_new)
        outs[1].append(c_new)
        outs[2].append(jnp.concatenate([cx, cb_, cc], axis=-1))
        outs[3].append(s_new)

        proj_s = _matmul_nt(hs, w_t, l, N_MAIN, 1024, "inproj_decode")
        dt_s = _matmul_nt(hs, w_dt_lane, l, SSM_DIM, SSM_DIM, "dtproj_decode")
        o_ret, ret_acc = _ret_decode(proj_s, state_ret, ret_acc, l, cos_s, sin_s, gam_tab, ret_norm[l][None, :],
                                     SUBLANES)
        o_sc, c_new = _sc_decode(proj_s, state_sconv[l].reshape(nb, (SC_WIDTH - 1) * SC_DIM), sc_conv_w[l],
                                 sc_conv_b[l][None, :])
        y_s, cs_new, ssm_acc = _ssd_decode(
            proj_s.reshape(nb, N_MAIN // LANES, LANES), dt_s.reshape(nb, xr, LANES),
            state_ssm_conv.reshape(depth, nb, SSM_CONV - 1, cr, LANES), state_ssm, ssm_acc, l,
            ssm_conv_w[l].reshape(SSM_CONV, cr, LANES), conv_b.reshape(1, cr, LANES),
            _per_lane(ssm_dt_bias[l]).reshape(1, xr, LANES), _per_lane(ssm_a_log[l]).reshape(1, xr, LANES),
            dsk_lane.reshape(1, xr, LANES), 4)
        xs, hs = _outproj(o_ret, o_sc, y_s.reshape(nb, SSM_DIM), xs, w_out16, l, ssm_norm[l][None, :],
                          norm_post[l][None, :], g_next, nb)
        outs[4].append(c_new.reshape(nb, SC_WIDTH - 1, SC_DIM))
        outs[5].append(cs_new.reshape(nb, SSM_CONV - 1, SSM_CONV_DIM))

    stacked = [jnp.stack(o) for o in outs]
    return (xp.reshape(batch, seq, D_MODEL), xs.reshape(nb, 1, D_MODEL), *stacked[:4],
            ret_acc, stacked[4], stacked[5], ssm_acc)
```

```python
import functools

import jax
import jax.numpy as jnp
import numpy as np
from jax import lax
from jax.experimental import pallas as pl
from jax.experimental.pallas import tpu as pltpu

F32 = jnp.float32
BF16 = jnp.bfloat16

D_MODEL = 2048
D_MIX = 2 * D_MODEL
RET_HEADS = 8
RET_DIM = 1024
HEAD_DIM = 128
SC_DIM = 1024
SC_WIDTH = 3
SSM_DIM = 2048
SSM_HEAD_DIM = 64
SSM_HEADS = 32
SSM_GROUPS = 4
SSM_STATE = 128
SSM_CONV = 4
SSM_CONV_DIM = SSM_DIM + 2 * SSM_GROUPS * SSM_STATE
CHUNK = 128
PROJ_ROWS = 2 * CHUNK
ROPE_BASE = 10000.0
EPS = 1e-6
PAST_LEN = 16384

OFF_Q, OFF_K, OFF_V, OFF_GR = 0, 1024, 2048, 3072
OFF_BG, OFF_CG, OFF_SH, OFF_GS = 4096, 5120, 6144, 7168
OFF_Z, OFF_X, OFF_B, OFF_C, OFF_DT = 8192, 10240, 12288, 12800, 13312
N_MAIN = OFF_DT

LANES = 128
SUBLANES = 8
V7X_VMEM_LIMIT = 60 * 1024 * 1024
PAIR = 2 * SSM_HEAD_DIM
HEADS_PER_GROUP = SSM_HEADS // SSM_GROUPS
PAIRS_PER_GROUP = HEADS_PER_GROUP // 2
N_PAIRS = SSM_HEADS // 2
ITEMS = 16


def _params(*sem):
    return pltpu.CompilerParams(dimension_semantics=sem, vmem_limit_bytes=V7X_VMEM_LIMIT)


def _silu(x):
    return x * jax.nn.sigmoid(x)


def _softplus(x):
    return jnp.maximum(x, 0.0) + jnp.log1p(jnp.exp(-jnp.abs(x)))


def _dot(a, b):
    return jnp.dot(a, b, preferred_element_type=F32)


def _bf16_parts(x, n):
    parts = []
    for _ in range(n):
        p = x.astype(BF16).astype(F32)
        parts.append(p)
        x = x - p
    return parts


def _dot_nt(a, b):
    return lax.dot_general(a, b, (((1,), (1,)), ((), ())), preferred_element_type=F32)


def _prenorm_kernel(x_ref, g_ref, o_ref):
    x = x_ref[...]
    ms = jnp.mean(x * x, axis=-1, keepdims=True)
    o_ref[...] = (x * lax.rsqrt(ms + EPS) * g_ref[...]).astype(o_ref.dtype)


def _prenorm(x, g, tm):
    m, d = x.shape
    return pl.pallas_call(
        _prenorm_kernel,
        out_shape=jax.ShapeDtypeStruct((m, d), BF16),
        grid=(m // tm,),
        in_specs=[pl.BlockSpec((tm, d), lambda i: (i, 0)), pl.BlockSpec((1, d), lambda i: (0, 0))],
        out_specs=pl.BlockSpec((tm, d), lambda i: (i, 0)),
        compiler_params=_params("parallel"),
        name="prenorm",
    )(x, g)


def _mm_nt_kernel(x_ref, w_ref, o_ref):
    o_ref[...] = _dot_nt(x_ref[...], w_ref[...].astype(BF16))


def _matmul_nt(x, w_t, layer, n_cols, tn, name):
    m, k = x.shape
    return pl.pallas_call(
        _mm_nt_kernel,
        out_shape=jax.ShapeDtypeStruct((m, n_cols), F32),
        grid=(n_cols // tn,),
        in_specs=[pl.BlockSpec((m, k), lambda j: (0, 0)),
                  pl.BlockSpec((None, tn, k), lambda j: (layer, j, 0))],
        out_specs=pl.BlockSpec((m, tn), lambda j: (0, j)),
        compiler_params=_params("parallel"),
        name=name,
    )(x, w_t)


def _outproj_kernel(oret_ref, osc_ref, ssm_ref, x_ref, w_ref, gssm_ref, gpost_ref, gnext_ref, y_ref, hn_ref):
    ypre = ssm_ref[...]
    ms = jnp.mean(ypre * ypre, axis=-1, keepdims=True)
    ossm = (ypre * lax.rsqrt(ms + EPS) * gssm_ref[...]).astype(BF16)
    acc = _dot(oret_ref[...], w_ref[0:RET_DIM, :])
    acc = acc + _dot(osc_ref[...], w_ref[RET_DIM:RET_DIM + SC_DIM, :])
    acc = acc + _dot(ossm, w_ref[RET_DIM + SC_DIM:, :])
    ms2 = jnp.mean(acc * acc, axis=-1, keepdims=True)
    y = x_ref[...] + acc * lax.rsqrt(ms2 + EPS) * gpost_ref[...]
    y_ref[...] = y
    ms3 = jnp.mean(y * y, axis=-1, keepdims=True)
    hn_ref[...] = (y * lax.rsqrt(ms3 + EPS) * gnext_ref[...]).astype(hn_ref.dtype)


def _outproj(oret, osc, ssm_pre, x, w_out, layer, g_ssm, g_post, g_next, tm):
    m = x.shape[0]
    row = lambda width: pl.BlockSpec((tm, width), lambda i: (i, 0))
    vec = lambda width: pl.BlockSpec((1, width), lambda i: (0, 0))
    return pl.pallas_call(
        _outproj_kernel,
        out_shape=(jax.ShapeDtypeStruct((m, D_MODEL), F32), jax.ShapeDtypeStruct((m, D_MODEL), BF16)),
        grid=(m // tm,),
        in_specs=[row(RET_DIM), row(SC_DIM), row(SSM_DIM), row(D_MODEL),
                  pl.BlockSpec((None, D_MIX, D_MODEL), lambda i: (layer, 0, 0), pipeline_mode=pl.Buffered(1)),
                  vec(SSM_DIM), vec(D_MODEL), vec(D_MODEL)],
        out_specs=(row(D_MODEL), row(D_MODEL)),
        compiler_params=_params("parallel"),
        name="outproj",
    )(oret, osc, ssm_pre, x, w_out, g_ssm, g_post, g_next)


def _rope(t, cos2, sin2):
    return t * cos2 + pltpu.roll(t, HEAD_DIM // 2, 1) * sin2


def _head_norm_gate(o, g_row, gate):
    mu = jnp.mean(o, axis=-1, keepdims=True)
    oc = o - mu
    var = jnp.mean(oc * oc, axis=-1, keepdims=True)
    return oc * lax.rsqrt(var + EPS) * g_row * _silu(gate)


def _shift_rows(tail, u, s):
    ext = jnp.concatenate([tail, u], axis=0)
    return ext[SUBLANES - s:SUBLANES - s + u.shape[0]]


def _causal_conv_rows(tail, u, w_ref, b_ref):
    width = w_ref.shape[0]
    out = b_ref[...]
    for j in range(width - 1):
        out = out + _shift_rows(tail, u, width - 1 - j) * w_ref[j:j + 1, :]
    return out + u * w_ref[width - 1:width, :]


def _load_weight_cols(w_refs, w16_ref):
    off = 0
    for w_ref in w_refs:
        n = w_ref.shape[0]
        for r in range(0, n, LANES):
            blk = w_ref[r:min(r + LANES, n), :]
            if blk.shape[0] < LANES:
                blk = jnp.concatenate([blk, jnp.zeros((LANES - blk.shape[0], blk.shape[1]), F32)], axis=0)
            w16_ref[:, off:off + LANES] = jnp.transpose(blk).astype(BF16)
            off += LANES


def _next_head_spec(batch, seq, proj_rows):
    per_seq = seq // proj_rows
    return pl.BlockSpec((proj_rows, D_MODEL), lambda c, b: (jnp.minimum(b + 1, batch - 1) * per_seq, 0),
                        pipeline_mode=pl.Buffered(1))


def _projected_chunks(h_ref, hn_ref, w16_ref, p_even, p_odd, chunk_fn, carry):
    proj_rows = p_even.shape[0]
    assert h_ref.shape[0] % proj_rows == 0 and proj_rows % CHUNK == 0
    steps = h_ref.shape[0] // proj_rows
    per_step = proj_rows // CHUNK

    def project(step, p_ref):
        rows = pl.ds(pl.multiple_of(step * proj_rows, proj_rows), proj_rows)
        p_ref[...] = _dot(h_ref[rows, :], w16_ref[...])

    def mix(step, p_ref, carry):
        for c in range(per_step):
            carry = chunk_fn(p_ref, c * CHUNK, step * per_step + c, carry)
        return carry

    def body(i, carry):
        project(2 * i + 1, p_odd)
        carry = mix(2 * i, p_even, carry)
        project(2 * i + 2, p_even)
        return mix(2 * i + 1, p_odd, carry)

    @pl.when(pl.program_id(1) == 0)
    def _():
        project(0, p_even)

    full = (steps - 1) // 2
    carry = lax.fori_loop(0, full, body, carry)
    if steps - 2 * full == 2:
        project(2 * full + 1, p_odd)
        carry = mix(2 * full, p_even, carry)
        p_even[...] = _dot(hn_ref[...], w16_ref[...])
        return mix(2 * full + 1, p_odd, carry)
    p_odd[...] = _dot(hn_ref[...], w16_ref[...])
    carry = mix(2 * full, p_even, carry)
    p_even[...] = p_odd[...]
    return carry


def _ret_prompt_kernel(h_ref, hn_ref, wq_ref, wk_ref, wv_ref, wg_ref, cos_ref, sin_ref, dmat_ref, qdec_ref, kdec_ref,
                       cdec_ref, gret_ref, o_ref, st_ref, w16_ref, p_even, p_odd, s_scr):
    @pl.when(pl.program_id(1) == 0)
    def _():
        _load_weight_cols((wq_ref, wk_ref, wv_ref, wg_ref), w16_ref)

    width = wq_ref.shape[0]
    heads = width // HEAD_DIM
    s_scr[...] = jnp.zeros_like(s_scr)

    def chunk(p, r0, n, carry):
        rows = pl.ds(pl.multiple_of(n * CHUNK, CHUNK), CHUNK)
        loc = slice(r0, r0 + CHUNK)
        cos2 = cos_ref[rows, :]
        sin2 = sin_ref[rows, :]
        for hh in range(heads):
            sl = slice(hh * HEAD_DIM, (hh + 1) * HEAD_DIM)
            part = lambda i: p[loc, i * width + hh * HEAD_DIM:i * width + (hh + 1) * HEAD_DIM]
            qr = _rope(part(0), cos2, sin2)
            kr = _rope(part(1), cos2, sin2) * (HEAD_DIM ** -0.5)
            qb = qr.astype(BF16)
            vb = part(2).astype(BF16)
            scores = _dot_nt(qb, kr.astype(BF16)) * dmat_ref[hh]
            s_old = s_scr[hh]
            lhs = jnp.concatenate([scores.astype(BF16), (qr * qdec_ref[hh]).astype(BF16)], axis=1)
            o = _dot(lhs, jnp.concatenate([vb, s_old.astype(BF16)], axis=0))
            kd_t = jnp.transpose(kr * kdec_ref[hh]).astype(BF16)
            s_scr[hh] = cdec_ref[hh] * s_old + _dot(kd_t, vb)
            o_ref[rows, sl] = _head_norm_gate(o, gret_ref[:, sl], part(3)).astype(o_ref.dtype)
        return carry

    _projected_chunks(h_ref, hn_ref, w16_ref, p_even, p_odd, chunk, 0)
    st_ref[0] = s_scr[...]


def _ret_prompt(h, w_t, layer, batch, seq, cos2, sin2, tabs, g_ret):
    hp = 4
    width = hp * HEAD_DIM
    wrow = lambda off: pl.BlockSpec((None, width, D_MODEL), lambda c, b, off=off: (layer, off // width + c, 0),
                                    pipeline_mode=pl.Buffered(1))
    tab = pl.BlockSpec((hp, CHUNK, CHUNK), lambda c, b: (c, 0, 0), pipeline_mode=pl.Buffered(1))
    full = pl.BlockSpec((seq, HEAD_DIM), lambda c, b: (0, 0), pipeline_mode=pl.Buffered(1))
    return pl.pallas_call(
        _ret_prompt_kernel,
        out_shape=(jax.ShapeDtypeStruct((batch * seq, RET_DIM), BF16),
                   jax.ShapeDtypeStruct((batch, RET_HEADS, HEAD_DIM, HEAD_DIM), F32)),
        grid=(RET_HEADS // hp, batch),
        in_specs=[pl.BlockSpec((seq, D_MODEL), lambda c, b: (b, 0)), _next_head_spec(batch, seq, PROJ_ROWS),
                  wrow(OFF_Q), wrow(OFF_K), wrow(OFF_V), wrow(OFF_GR), full, full, tab, tab, tab, tab,
                  pl.BlockSpec((1, width), lambda c, b: (0, c))],
        out_specs=(pl.BlockSpec((seq, width), lambda c, b: (b, c), pipeline_mode=pl.Buffered(1)),
                   pl.BlockSpec((1, hp, HEAD_DIM, HEAD_DIM), lambda c, b: (b, c, 0, 0))),
        scratch_shapes=[pltpu.VMEM((D_MODEL, 4 * width), BF16),
                        pltpu.VMEM((PROJ_ROWS, 4 * width), F32), pltpu.VMEM((PROJ_ROWS, 4 * width), F32),
                        pltpu.VMEM((hp, HEAD_DIM, HEAD_DIM), F32)],
        compiler_params=_params("parallel", "arbitrary"),
        name="ret_prompt",
    )(h, h, w_t, w_t, w_t, w_t, cos2, sin2, *tabs, g_ret)


def _ret_decode_kernel(q_ref, k_ref, v_ref, g_ref, cos_ref, sin_ref, gam_ref, gret_ref, s_ref, acc_ref, o_ref,
                       sn_ref):
    del acc_ref
    bt = q_ref.shape[0]
    assert 2 * bt == ITEMS
    cos2 = cos_ref[...]
    sin2 = sin_ref[...]
    lane = lax.broadcasted_iota(jnp.int32, (LANES, LANES), 1)
    zero = jnp.zeros((bt, HEAD_DIM), F32)
    one = jnp.ones((bt, HEAD_DIM), F32)
    for hp in range(RET_HEADS // 2):
        qr, kr, v, gam, sls = [], [], [], [], []
        for hh in range(2):
            h = 2 * hp + hh
            sl = slice(h * HEAD_DIM, (h + 1) * HEAD_DIM)
            sls.append(sl)
            gam.append(gam_ref[h][0:1, :])
            qr.append(_rope(q_ref[:, sl], cos2, sin2))
            kr.append(_rope(k_ref[:, sl], cos2, sin2) * (HEAD_DIM ** -0.5))
            v.append(v_ref[:, sl])
        k2 = [_bf16_parts(t, 2) for t in kr]
        q3 = [_bf16_parts(t, 3) for t in qr]
        v2 = [_bf16_parts(t, 2) for t in v]
        both = lambda parts, n: [parts[0][n], parts[1][n]]
        a = jnp.concatenate(both(k2, 0) + both(k2, 0) + both(k2, 1) + both(q3, 0) + both(q3, 1) + both(q3, 2)
                            + [zero] * 4, axis=0)
        cols = jnp.transpose(a)
        r_kv = jnp.concatenate(both(v2, 0) + both(v2, 1) + both(v2, 0) + [zero] * 10, axis=0)
        r_q = jnp.concatenate([zero] * 6 + [one] * 6 + [zero] * 4, axis=0)
        r = jnp.concatenate([r_kv, r_q], axis=1).astype(BF16)
        q_s = [[], []]
        for i in range(ITEMS):
            hh, j = divmod(i, bt)
            h = 2 * hp + hh
            out = _dot(jnp.where((lane & (ITEMS - 1)) == i, cols, 0.0).astype(BF16), r)
            s_old = s_ref[j, h]
            sn_ref[j, h] = gam[hh] * s_old + out[:, :HEAD_DIM]
            q_s[hh].append(jnp.sum(s_old * out[:, HEAD_DIM:], axis=0, keepdims=True))
        for hh in range(2):
            qk = jnp.sum(qr[hh] * kr[hh], axis=-1, keepdims=True)
            o = qk * v[hh] + jnp.concatenate(q_s[hh], axis=0) * gam[hh]
            o_ref[:, sls[hh]] = _head_norm_gate(o, gret_ref[:, sls[hh]], g_ref[:, sls[hh]]).astype(o_ref.dtype)


def _ret_decode(proj, states, acc, layer, cos2, sin2, gam_tab, g_ret, bt):
    nb = proj.shape[0]
    col = lambda off: pl.BlockSpec((bt, RET_DIM), lambda i, off=off: (i, off // RET_DIM))
    slab = pl.BlockSpec((None, bt, RET_HEADS, HEAD_DIM, HEAD_DIM), lambda i: (layer, i, 0, 0, 0))
    return pl.pallas_call(
        _ret_decode_kernel,
        out_shape=(jax.ShapeDtypeStruct((nb, RET_DIM), BF16), jax.ShapeDtypeStruct(states.shape, F32)),
        grid=(nb // bt,),
        in_specs=[col(OFF_Q), col(OFF_K), col(OFF_V), col(OFF_GR),
                  pl.BlockSpec((1, HEAD_DIM), lambda i: (0, 0)), pl.BlockSpec((1, HEAD_DIM), lambda i: (0, 0)),
                  pl.BlockSpec((RET_HEADS, SUBLANES, LANES), lambda i: (0, 0, 0)),
                  pl.BlockSpec((1, RET_DIM), lambda i: (0, 0)), slab, pl.BlockSpec(memory_space=pl.ANY)],
        out_specs=(pl.BlockSpec((bt, RET_DIM), lambda i: (i, 0)), slab),
        input_output_aliases={9: 1},
        compiler_params=_params("parallel"),
        name="ret_decode",
    )(proj, proj, proj, proj, cos2, sin2, gam_tab, g_ret, states, acc)


def _sc_prompt_kernel(h_ref, hn_ref, wbg_ref, wcg_ref, wsh_ref, wgs_ref, w_ref, b_ref, o_ref, st_ref, w16_ref,
                      p_even, p_odd):
    @pl.when(pl.program_id(1) == 0)
    def _():
        _load_weight_cols((wbg_ref, wcg_ref, wsh_ref, wgs_ref), w16_ref)

    width = wbg_ref.shape[0]

    def chunk(p, r0, n, tail):
        rows = pl.ds(pl.multiple_of(n * CHUNK, CHUNK), CHUNK)
        part = lambda i: p[r0:r0 + CHUNK, i * width:(i + 1) * width]
        u = part(1) * part(2)
        conv = _causal_conv_rows(tail, u, w_ref, b_ref)
        o_ref[rows, :] = (part(0) * conv * _silu(part(3))).astype(o_ref.dtype)
        return u[CHUNK - SUBLANES:, :]

    tail = _projected_chunks(h_ref, hn_ref, w16_ref, p_even, p_odd, chunk, jnp.zeros((SUBLANES, width), F32))
    st_ref[0] = tail[SUBLANES - (SC_WIDTH - 1):, :]


def _sc_prompt(h, w_t, layer, batch, seq, w, b):
    width = 512
    proj_rows = min(2 * PROJ_ROWS, seq)
    wrow = lambda off: pl.BlockSpec((None, width, D_MODEL), lambda c, bi, off=off: (layer, off // width + c, 0),
                                    pipeline_mode=pl.Buffered(1))
    return pl.pallas_call(
        _sc_prompt_kernel,
        out_shape=(jax.ShapeDtypeStruct((batch * seq, SC_DIM), BF16),
                   jax.ShapeDtypeStruct((batch, SC_WIDTH - 1, SC_DIM), F32)),
        grid=(SC_DIM // width, batch),
        in_specs=[pl.BlockSpec((seq, D_MODEL), lambda c, bi: (bi, 0)), _next_head_spec(batch, seq, proj_rows),
                  wrow(OFF_BG), wrow(OFF_CG), wrow(OFF_SH), wrow(OFF_GS),
                  pl.BlockSpec((SC_WIDTH, width), lambda c, bi: (0, c)),
                  pl.BlockSpec((1, width), lambda c, bi: (0, c))],
        out_specs=(pl.BlockSpec((seq, width), lambda c, bi: (bi, c)),
                   pl.BlockSpec((1, SC_WIDTH - 1, width), lambda c, bi: (bi, 0, c))),
        scratch_shapes=[pltpu.VMEM((D_MODEL, 4 * width), BF16),
                        pltpu.VMEM((proj_rows, 4 * width), F32), pltpu.VMEM((proj_rows, 4 * width), F32)],
        compiler_params=_params("parallel", "arbitrary"),
        name="sc_prompt",
    )(h, h, w_t, w_t, w_t, w_t, w, b)


def _sc_decode_kernel(bg_ref, cg_ref, sh_ref, gs_ref, st_ref, w_ref, b_ref, o_ref, stn_ref):
    u = cg_ref[...] * sh_ref[...]
    r0 = st_ref[:, 0:SC_DIM]
    r1 = st_ref[:, SC_DIM:]
    conv = b_ref[...] + r0 * w_ref[0:1, :] + r1 * w_ref[1:2, :] + u * w_ref[2:3, :]
    o_ref[...] = (bg_ref[...] * conv * _silu(gs_ref[...])).astype(o_ref.dtype)
    stn_ref[:, 0:SC_DIM] = r1
    stn_ref[:, SC_DIM:] = u


def _sc_decode(proj, state2d, w, b):
    nb = proj.shape[0]
    col = lambda off: pl.BlockSpec((nb, SC_DIM), lambda i, off=off: (0, off // SC_DIM))
    whole = lambda shape: pl.BlockSpec(shape, lambda i: (0,) * len(shape))
    return pl.pallas_call(
        _sc_decode_kernel,
        out_shape=(jax.ShapeDtypeStruct((nb, SC_DIM), BF16), jax.ShapeDtypeStruct(state2d.shape, F32)),
        grid=(1,),
        in_specs=[col(OFF_BG), col(OFF_CG), col(OFF_SH), col(OFF_GS), whole(state2d.shape),
                  whole((SC_WIDTH, SC_DIM)), whole((1, SC_DIM))],
        out_specs=(whole((nb, SC_DIM)), whole(state2d.shape)),
        compiler_params=_params("arbitrary"),
        name="sc_decode",
    )(proj, proj, proj, proj, state2d, w, b)


def _lane_cumsum(a, lane):
    s = 1
    while s < CHUNK:
        a = a + jnp.where(lane >= s, pltpu.roll(a, s, 1), 0.0)
        s *= 2
    return a


def _ssd_prompt_kernel(h_ref, hn_ref, wz_ref, wx_ref, wb_ref, wc_ref, wdt_ref, dtb_ref, alog_ref, dsk_ref,
                       cwx_ref, cwb_ref, cwc_ref, cbx_ref, cbb_ref, cbc_ref,
                       y_ref, stx_ref, stb_ref, stc_ref, sst_ref, w16_ref, p_even, p_odd, s_scr):
    @pl.when(pl.program_id(1) == 0)
    def _():
        _load_weight_cols((wz_ref, wx_ref, wb_ref, wc_ref, wdt_ref), w16_ref)

    gw = wz_ref.shape[0]
    n_st = wb_ref.shape[0]
    off_x, off_b, off_c, off_dt = gw, 2 * gw, 2 * gw + n_st, 2 * gw + 2 * n_st
    row = lax.broadcasted_iota(jnp.int32, (CHUNK, CHUNK), 0)
    lane = lax.broadcasted_iota(jnp.int32, (CHUNK, CHUNK), 1)
    lane8 = lax.broadcasted_iota(jnp.int32, (HEADS_PER_GROUP, CHUNK), 1)
    causal = row >= lane
    low_lanes = lane < SSM_HEAD_DIM
    low_rows = row < SSM_HEAD_DIM
    a_neg = -jnp.exp(alog_ref[0])
    dt_bias = dtb_ref[0]
    pad_heads = jnp.zeros((CHUNK - HEADS_PER_GROUP, CHUNK), F32)
    s_scr[...] = jnp.zeros_like(s_scr)

    def chunk(p, r0, n, tails):
        tx, tb, tc = tails
        rows = pl.ds(pl.multiple_of(n * CHUNK, CHUNK), CHUNK)
        loc = slice(r0, r0 + CHUNK)
        x_raw = p[loc, off_x:off_x + gw]
        b_raw = p[loc, off_b:off_b + n_st]
        c_raw = p[loc, off_c:off_c + n_st]
        xc = _silu(_causal_conv_rows(tx, x_raw, cwx_ref, cbx_ref))
        bb = _silu(_causal_conv_rows(tb, b_raw, cwb_ref, cbb_ref)).astype(BF16)
        cb16 = _silu(_causal_conv_rows(tc, c_raw, cwc_ref, cbc_ref)).astype(BF16)
        dt_t = _softplus(jnp.transpose(p[loc, off_dt:off_dt + LANES])[0:HEADS_PER_GROUP, :] + dt_bias)
        acum_t = _lane_cumsum(dt_t * a_neg, lane8)
        acum = jnp.transpose(jnp.concatenate([acum_t, pad_heads], axis=0))
        cb = _dot_nt(cb16, bb)
        y_intra, e_sel, dec_sel, xw, s_old = [], [], [], [], []
        for pp in range(PAIRS_PER_GROUP):
            x_pair = xc[:, pp * PAIR:(pp + 1) * PAIR]
            m, e_col, w_row, dec = [], [], [], []
            for hh in range(2):
                c = 2 * pp + hh
                a_col = jnp.broadcast_to(acum[:, c:c + 1], (CHUNK, CHUNK))
                a_row = acum_t[c:c + 1, :]
                dt_row = dt_t[c:c + 1, :]
                lmat = jnp.exp(jnp.where(causal, a_col - a_row, -jnp.inf))
                m.append((cb * lmat * dt_row).astype(BF16))
                e_col.append(jnp.exp(a_col))
                last = a_col[CHUNK - 1:CHUNK, :]
                w_row.append(jnp.exp(last - a_row) * dt_row)
                dec.append(jnp.exp(last))
            x_stack = jnp.concatenate([jnp.where(low_lanes, x_pair, 0.0), jnp.where(low_lanes, 0.0, x_pair)], axis=0)
            y_intra.append(_dot(jnp.concatenate(m, axis=1), x_stack.astype(BF16)))
            e_sel.append(jnp.where(low_lanes, e_col[0], e_col[1]))
            dec_sel.append(jnp.where(low_rows, dec[0], dec[1]))
            xw.append((jnp.transpose(x_pair) * jnp.where(low_rows, w_row[0], w_row[1])).astype(BF16))
            s_old.append(s_scr[pp])
        s_upd = _dot(jnp.concatenate(xw, axis=0), bb)
        for q in range(PAIRS_PER_GROUP // 2):
            s_two = jnp.concatenate([s_old[2 * q], s_old[2 * q + 1]], axis=0).astype(BF16)
            y_two = _dot_nt(cb16, s_two)
            for r in range(2):
                pp = 2 * q + r
                sl = slice(pp * PAIR, (pp + 1) * PAIR)
                y = y_intra[pp] + y_two[:, r * PAIR:(r + 1) * PAIR] * e_sel[pp] + dsk_ref[0][:, sl] * xc[:, sl]
                y_ref[rows, sl] = y * _silu(p[loc, sl])
                s_scr[pp] = dec_sel[pp] * s_old[pp] + s_upd[pp * PAIR:(pp + 1) * PAIR, :]
        cut = CHUNK - SUBLANES
        return x_raw[cut:, :], b_raw[cut:, :], c_raw[cut:, :]

    zeros = lambda width: jnp.zeros((SUBLANES, width), F32)
    tx, tb, tc = _projected_chunks(h_ref, hn_ref, w16_ref, p_even, p_odd, chunk,
                                   (zeros(gw), zeros(n_st), zeros(n_st)))
    keep = SUBLANES - (SSM_CONV - 1)
    stx_ref[0] = tx[keep:, :]
    stb_ref[0] = tb[keep:, :]
    stc_ref[0] = tc[keep:, :]
    for pp in range(PAIRS_PER_GROUP):
        for hh in range(2):
            sst_ref[0, 2 * pp + hh] = s_scr[pp, hh * SSM_HEAD_DIM:(hh + 1) * SSM_HEAD_DIM, :]


def _ssd_prompt(h, w_t, layer, batch, seq, dtb, alog, dsk, conv_w, conv_b):
    gw = SSM_DIM // SSM_GROUPS
    n = SSM_STATE
    wrow = lambda off, width: pl.BlockSpec((None, width, D_MODEL),
                                           lambda g, b, off=off, width=width: (layer, off // width + g, 0),
                                           pipeline_mode=pl.Buffered(1))
    grp = lambda rows, width: pl.BlockSpec((1, rows, width), lambda g, b: (g, 0, 0))
    cw = lambda off, width: pl.BlockSpec((SSM_CONV, width), lambda g, b, off=off, width=width: (0, off // width + g))
    cbias = lambda off, width: pl.BlockSpec((1, width), lambda g, b, off=off, width=width: (0, off // width + g))
    st = lambda width: pl.BlockSpec((1, SSM_CONV - 1, width), lambda g, b: (b, 0, g))
    n_proj = 2 * gw + 2 * n + LANES
    return pl.pallas_call(
        _ssd_prompt_kernel,
        out_shape=(jax.ShapeDtypeStruct((batch * seq, SSM_DIM), F32),
                   jax.ShapeDtypeStruct((batch, SSM_CONV - 1, SSM_DIM), F32),
                   jax.ShapeDtypeStruct((batch, SSM_CONV - 1, SSM_GROUPS * n), F32),
                   jax.ShapeDtypeStruct((batch, SSM_CONV - 1, SSM_GROUPS * n), F32),
                   jax.ShapeDtypeStruct((batch, SSM_HEADS, SSM_HEAD_DIM, n), F32)),
        grid=(SSM_GROUPS, batch),
        in_specs=[pl.BlockSpec((seq, D_MODEL), lambda g, b: (b, 0)), _next_head_spec(batch, seq, PROJ_ROWS),
                  wrow(OFF_Z, gw), wrow(OFF_X, gw), wrow(OFF_B, n), wrow(OFF_C, n), wrow(OFF_DT, HEADS_PER_GROUP),
                  grp(HEADS_PER_GROUP, CHUNK), grp(HEADS_PER_GROUP, CHUNK), grp(1, gw),
                  cw(0, gw), cw(SSM_DIM, n), cw(SSM_DIM + SSM_GROUPS * n, n),
                  cbias(0, gw), cbias(SSM_DIM, n), cbias(SSM_DIM + SSM_GROUPS * n, n)],
        out_specs=(pl.BlockSpec((seq, gw), lambda g, b: (b, g)), st(gw), st(n), st(n),
                   pl.BlockSpec((1, HEADS_PER_GROUP, SSM_HEAD_DIM, n), lambda g, b: (b, g, 0, 0))),
        scratch_shapes=[pltpu.VMEM((D_MODEL, n_proj), BF16),
                        pltpu.VMEM((PROJ_ROWS, n_proj), F32), pltpu.VMEM((PROJ_ROWS, n_proj), F32),
                        pltpu.VMEM((PAIRS_PER_GROUP, PAIR, n), F32)],
        compiler_params=_params("parallel", "arbitrary"),
        name="ssd_prompt",
    )(h, h, w_t, w_t, w_t, w_t, w_t, dtb, alog, dsk, conv_w, conv_w, conv_w, conv_b, conv_b, conv_b)


def _ssd_decode_kernel(z_ref, x_ref, bc_ref, dtr_ref, cst_ref, s_ref, cw_ref, cb_ref, dtb_ref, alog_ref, dsk_ref,
                       acc_ref, y_ref, cstn_ref, sn_ref):
    del acc_ref
    bt = z_ref.shape[0]
    xr = SSM_DIM // LANES
    gr = SSM_GROUPS * SSM_STATE // LANES
    u = jnp.concatenate([x_ref[...], bc_ref[...]], axis=1)
    conv = cb_ref[...]
    for j in range(SSM_CONV - 1):
        conv = conv + cst_ref[:, j] * cw_ref[j]
        cstn_ref[:, j] = u if j == SSM_CONV - 2 else cst_ref[:, j + 1]
    act = _silu(conv + u * cw_ref[SSM_CONV - 1])
    xs = act[:, 0:xr]
    bm = act[:, xr:xr + gr]
    cm = act[:, xr + gr:]
    dt = _softplus(dtr_ref[...] + dtb_ref[...])
    ea = jnp.exp(dt * -jnp.exp(alog_ref[...]))
    xdt = xs * dt
    assert xr == ITEMS == N_PAIRS
    lane = lax.broadcasted_iota(jnp.int32, (LANES, LANES), 1)
    row16 = lax.broadcasted_iota(jnp.int32, (xr, LANES), 0)
    zeros = lambda n: jnp.zeros((n, LANES), F32)
    r_ea = jnp.concatenate([zeros(3 * ITEMS), jnp.ones((3 * ITEMS, LANES), F32), zeros(2 * ITEMS)], axis=0)
    for j in range(bt):
        cbt = zeros(xr)
        b_rows = zeros(xr)
        for g in range(SSM_GROUPS):
            in_group = row16 // PAIRS_PER_GROUP == g
            cb_g = jnp.sum(cm[j, g:g + 1, :] * bm[j, g:g + 1, :], axis=-1, keepdims=True)
            cbt = jnp.where(in_group, cb_g, cbt)
            b_rows = jnp.where(in_group, bm[j, g:g + 1, :], b_rows)
        xd2 = _bf16_parts(xdt[j], 2)
        ea3 = _bf16_parts(ea[j], 3)
        b2 = _bf16_parts(b_rows, 2)
        a = jnp.concatenate([xd2[0], xd2[0], xd2[1]] + ea3 + [zeros(2 * ITEMS)], axis=0)
        cols = jnp.transpose(a)
        r_xb = jnp.concatenate([b2[0], b2[1], b2[0], zeros(5 * ITEMS)], axis=0)
        r = jnp.concatenate([r_xb, r_ea], axis=1).astype(BF16)
        y_t = jnp.zeros((LANES, LANES), F32)
        for pp in range(N_PAIRS):
            g = pp // PAIRS_PER_GROUP
            out = _dot(jnp.where((lane & (ITEMS - 1)) == pp, cols, 0.0).astype(BF16), r)
            s_old = s_ref[j, 2 * pp:2 * pp + 2].reshape(PAIR, SSM_STATE)
            y_col = jnp.sum(s_old * cm[j, g:g + 1, :], axis=-1, keepdims=True)
            y_t = jnp.where(lane == pp, y_col, y_t)
            s_new = out[:, SSM_STATE:] * s_old + out[:, :SSM_STATE]
            sn_ref[j, 2 * pp:2 * pp + 2] = s_new.reshape(2, SSM_HEAD_DIM, SSM_STATE)
        y_inter = jnp.transpose(y_t)[0:xr, :]
        y = cbt * xdt[j] + y_inter * ea[j] + dsk_ref[0] * xs[j]
        y_ref[j] = y * _silu(z_ref[j])


def _ssd_decode(proj3, dt_raw3, cstates, states, acc, layer, conv_w3, conv_b3, dtb3, alog3, dsk3, bt):
    nb = proj3.shape[0]
    xr = SSM_DIM // LANES
    cr = SSM_CONV_DIM // LANES
    whole = lambda shape: pl.BlockSpec(shape, lambda i: (0,) * len(shape))
    cst = pl.BlockSpec((bt, SSM_CONV - 1, cr, LANES), lambda i: (i, 0, 0, 0))
    st_in = pl.BlockSpec((None, bt, SSM_HEADS, SSM_HEAD_DIM, SSM_STATE), lambda i: (layer, i, 0, 0, 0))
    cst_in = pl.BlockSpec((None, bt, SSM_CONV - 1, cr, LANES), lambda i: (layer, i, 0, 0, 0))
    return pl.pallas_call(
        _ssd_decode_kernel,
        out_shape=(jax.ShapeDtypeStruct((nb, xr, LANES), F32), jax.ShapeDtypeStruct(cstates.shape[1:], F32),
                   jax.ShapeDtypeStruct(states.shape, F32)),
        grid=(nb // bt,),
        in_specs=[pl.BlockSpec((bt, xr, LANES), lambda i: (i, OFF_Z // SSM_DIM, 0)),
                  pl.BlockSpec((bt, xr, LANES), lambda i: (i, OFF_X // SSM_DIM, 0)),
                  pl.BlockSpec((bt, SUBLANES, LANES), lambda i: (i, OFF_B // (SUBLANES * LANES), 0)),
                  pl.BlockSpec((bt, xr, LANES), lambda i: (i, 0, 0)),
                  cst_in, st_in, whole((SSM_CONV, cr, LANES)), whole((1, cr, LANES)),
                  whole((1, xr, LANES)), whole((1, xr, LANES)), whole((1, xr, LANES)),
                  pl.BlockSpec(memory_space=pl.ANY)],
        out_specs=(pl.BlockSpec((bt, xr, LANES), lambda i: (i, 0, 0)), cst, st_in),
        input_output_aliases={11: 2},
        compiler_params=_params("parallel"),
        name="ssd_decode",
    )(proj3, proj3, proj3, dt_raw3, cstates, states, conv_w3, conv_b3, dtb3, alog3, dsk3, acc)


def _rope_tables(pos):
    half = HEAD_DIM // 2
    inv = ROPE_BASE ** (-jnp.arange(half, dtype=F32) / half)
    ang = pos.astype(F32)[:, None] * inv[None, :]
    cos, sin = jnp.cos(ang), jnp.sin(ang)
    return jnp.concatenate([cos, cos], axis=-1), jnp.concatenate([-sin, sin], axis=-1)


def _retention_tables():
    lg = jnp.log(1.0 - 2.0 ** (-5.0 - jnp.arange(RET_HEADS, dtype=F32)))
    i = jnp.arange(CHUNK, dtype=F32)
    diff = i[:, None] - i[None, :]
    dmat = jnp.exp(jnp.where((diff >= 0)[None], diff[None] * lg[:, None, None], -jnp.inf))
    full = lambda t: jnp.broadcast_to(t, (RET_HEADS, CHUNK, CHUNK))
    q_dec = full(jnp.exp((i + 1.0)[None, :, None] * lg[:, None, None]))
    k_dec = full(jnp.exp((CHUNK - 1.0 - i)[None, :, None] * lg[:, None, None]))
    c_dec = full(jnp.exp(CHUNK * lg)[:, None, None])
    gam = jnp.broadcast_to(jnp.exp(lg)[:, None, None], (RET_HEADS, SUBLANES, LANES))
    return (dmat, q_dec, k_dec, c_dec), gam


def _per_group(v):
    v = v.astype(F32).reshape(SSM_GROUPS, HEADS_PER_GROUP, 1)
    return jnp.broadcast_to(v, (SSM_GROUPS, HEADS_PER_GROUP, CHUNK))


def _per_lane(v):
    return jnp.repeat(v.astype(F32), SSM_HEAD_DIM)[None, :]


def kernel(x_prompt, x_sample, state_ret, state_sconv, state_ssm_conv, state_ssm, w_in, w_out, norm_pre, norm_post,
           ret_norm, sc_conv_w, sc_conv_b, ssm_conv_w, ssm_conv_b, ssm_dt_bias, ssm_a_log, ssm_d, ssm_norm):
    batch, seq, _ = x_prompt.shape
    nb = x_sample.shape[0]
    depth = w_in.shape[0]
    mp = batch * seq
    xr = SSM_DIM // LANES
    cr = SSM_CONV_DIM // LANES

    cos_p, sin_p = _rope_tables(jnp.arange(seq))
    cos_s, sin_s = _rope_tables(PAST_LEN + jnp.arange(1))
    ret_tabs, gam_tab = _retention_tables()

    w_out16 = w_out.astype(BF16)
    w_t = jnp.swapaxes(w_in, 1, 2)
    w_dt_lane = jnp.repeat(w_t[:, OFF_DT:, :], SSM_HEAD_DIM, axis=1).astype(BF16)

    tm_o = min(256, mp)
    xp = x_prompt.reshape(mp, D_MODEL)
    xs = x_sample.reshape(nb, D_MODEL)
    hp = _prenorm(xp, norm_pre[0][None, :], tm_o)
    hs = _prenorm(xs, norm_pre[0][None, :], nb)

    outs = [[] for _ in range(6)]
    ret_acc = jnp.zeros(state_ret.shape, F32)
    ssm_acc = jnp.zeros(state_ssm.shape, F32)
    for l in range(depth):
        g_next = norm_pre[(l + 1) % depth][None, :]
        dtb_g, alog_g = _per_group(ssm_dt_bias[l]), _per_group(ssm_a_log[l])
        dsk_lane = _per_lane(ssm_d[l])
        dsk_g = dsk_lane.reshape(SSM_GROUPS, 1, SSM_DIM // SSM_GROUPS)
        conv_b = ssm_conv_b[l][None, :]

        o_ret, r_new = _ret_prompt(hp, w_t, l, batch, seq, cos_p, sin_p, ret_tabs, ret_norm[l][None, :])
        o_sc, c_new = _sc_prompt(hp, w_t, l, batch, seq, sc_conv_w[l], sc_conv_b[l][None, :])
        ssm_pre, cx, cb_, cc, s_new = _ssd_prompt(hp, w_t, l, batch, seq, dtb_g, alog_g, dsk_g, ssm_conv_w[l], conv_b)
        xp, hp = _outproj(o_ret, o_sc, ssm_pre, xp, w_out16, l, ssm_norm[l][None, :], norm_post[l][None, :],
                          g_next, tm_o)
        outs[0].append(r_new)
        outs[1].append(c_new)
        outs[2].append(jnp.concatenate([cx, cb_, cc], axis=-1))
        outs[3].append(s_new)

        proj_s = _matmul_nt(hs, w_t, l, N_MAIN, 1024, "inproj_decode")
        dt_s = _matmul_nt(hs, w_dt_lane, l, SSM_DIM, SSM_DIM, "dtproj_decode")
        o_ret, ret_acc = _ret_decode(proj_s, state_ret, ret_acc, l, cos_s, sin_s, gam_tab, ret_norm[l][None, :],
                                     SUBLANES)
        o_sc, c_new = _sc_decode(proj_s, state_sconv[l].reshape(nb, (SC_WIDTH - 1) * SC_DIM), sc_conv_w[l],
                                 sc_conv_b[l][None, :])
        y_s, cs_new, ssm_acc = _ssd_decode(
            proj_s.reshape(nb, N_MAIN // LANES, LANES), dt_s.reshape(nb, xr, LANES),
            state_ssm_conv.reshape(depth, nb, SSM_CONV - 1, cr, LANES), state_ssm, ssm_acc, l,
            ssm_conv_w[l].reshape(SSM_CONV, cr, LANES), conv_b.reshape(1, cr, LANES),
            _per_lane(ssm_dt_bias[l]).reshape(1, xr, LANES), _per_lane(ssm_a_log[l]).reshape(1, xr, LANES),
            dsk_lane.reshape(1, xr, LANES), 4)
        xs, hs = _outproj(o_ret, o_sc, y_s.reshape(nb, SSM_DIM), xs, w_out16, l, ssm_norm[l][None, :],
                          norm_post[l][None, :], g_next, nb)
        outs[4].append(c_new.reshape(nb, SC_WIDTH - 1, SC_DIM))
        outs[5].append(cs_new.reshape(nb, SSM_CONV - 1, SSM_CONV_DIM))

    stacked = [jnp.stack(o) for o in outs]
    return (xp.reshape(batch, seq, D_MODEL), xs.reshape(nb, 1, D_MODEL), *stacked[:4],
            ret_acc, stacked[4], stacked[5], ssm_acc)
```

```python
import functools

import jax
import jax.numpy as jnp
import numpy as np
from jax import lax
from jax.experimental import pallas as pl
from jax.experimental.pallas import tpu as pltpu

F32 = jnp.float32
BF16 = jnp.bfloat16

D_MODEL = 2048
D_MIX = 2 * D_MODEL
RET_HEADS = 8
RET_DIM = 1024
HEAD_DIM = 128
SC_DIM = 1024
SC_WIDTH = 3
SSM_DIM = 2048
SSM_HEAD_DIM = 64
SSM_HEADS = 32
SSM_GROUPS = 4
SSM_STATE = 128
SSM_CONV = 4
SSM_CONV_DIM = SSM_DIM + 2 * SSM_GROUPS * SSM_STATE
CHUNK = 128
PROJ_ROWS = 2 * CHUNK
OUT_SUB_ROWS = 256
ROPE_BASE = 10000.0
EPS = 1e-6
PAST_LEN = 16384

OFF_Q, OFF_K, OFF_V, OFF_GR = 0, 1024, 2048, 3072
OFF_BG, OFF_CG, OFF_SH, OFF_GS = 4096, 5120, 6144, 7168
OFF_Z, OFF_X, OFF_B, OFF_C, OFF_DT = 8192, 10240, 12288, 12800, 13312
N_MAIN = OFF_DT

LANES = 128
SUBLANES = 8
V7X_VMEM_LIMIT = 60 * 1024 * 1024
PAIR = 2 * SSM_HEAD_DIM
HEADS_PER_GROUP = SSM_HEADS // SSM_GROUPS
PAIRS_PER_GROUP = HEADS_PER_GROUP // 2
N_PAIRS = SSM_HEADS // 2
ITEMS = 16


def _params(*sem):
    return pltpu.CompilerParams(dimension_semantics=sem, vmem_limit_bytes=V7X_VMEM_LIMIT)


def _silu(x):
    return x * jax.nn.sigmoid(x)


def _softplus(x):
    return jnp.maximum(x, 0.0) + jnp.log1p(jnp.exp(-jnp.abs(x)))


def _dot(a, b):
    return jnp.dot(a, b, preferred_element_type=F32)


def _bf16_parts(x, n):
    parts = []
    for _ in range(n):
        p = x.astype(BF16).astype(F32)
        parts.append(p)
        x = x - p
    return parts


def _dot_nt(a, b):
    return lax.dot_general(a, b, (((1,), (1,)), ((), ())), preferred_element_type=F32)


def _prenorm_kernel(x_ref, g_ref, o_ref):
    x = x_ref[...]
    ms = jnp.mean(x * x, axis=-1, keepdims=True)
    o_ref[...] = (x * lax.rsqrt(ms + EPS) * g_ref[...]).astype(o_ref.dtype)


def _prenorm(x, g, tm):
    m, d = x.shape
    return pl.pallas_call(
        _prenorm_kernel,
        out_shape=jax.ShapeDtypeStruct((m, d), BF16),
        grid=(m // tm,),
        in_specs=[pl.BlockSpec((tm, d), lambda i: (i, 0)), pl.BlockSpec((1, d), lambda i: (0, 0))],
        out_specs=pl.BlockSpec((tm, d), lambda i: (i, 0)),
        compiler_params=_params("parallel"),
        name="prenorm",
    )(x, g)


def _mm_nt_kernel(x_ref, w_ref, o_ref):
    o_ref[...] = _dot_nt(x_ref[...], w_ref[...].astype(BF16))


def _matmul_nt(x, w_t, layer, n_cols, tn, name):
    m, k = x.shape
    return pl.pallas_call(
        _mm_nt_kernel,
        out_shape=jax.ShapeDtypeStruct((m, n_cols), F32),
        grid=(n_cols // tn,),
        in_specs=[pl.BlockSpec((m, k), lambda j: (0, 0)),
                  pl.BlockSpec((None, tn, k), lambda j: (layer, j, 0))],
        out_specs=pl.BlockSpec((m, tn), lambda j: (0, j)),
        compiler_params=_params("parallel"),
        name=name,
    )(x, w_t)


def _outproj_kernel(oret_ref, osc_ref, ssm_ref, x_ref, w_ref, gssm_ref, gpost_ref, gnext_ref, y_ref, hn_ref):
    tm = x_ref.shape[0]
    sub = min(tm, OUT_SUB_ROWS)
    for r0 in range(0, tm, sub):
        rows = slice(r0, r0 + sub)
        ypre = ssm_ref[rows, :]
        ms = jnp.mean(ypre * ypre, axis=-1, keepdims=True)
        ossm = (ypre * lax.rsqrt(ms + EPS) * gssm_ref[...]).astype(BF16)
        acc = _dot(oret_ref[rows, :], w_ref[0:RET_DIM, :])
        acc = acc + _dot(osc_ref[rows, :], w_ref[RET_DIM:RET_DIM + SC_DIM, :])
        acc = acc + _dot(ossm, w_ref[RET_DIM + SC_DIM:, :])
        ms2 = jnp.mean(acc * acc, axis=-1, keepdims=True)
        y = x_ref[rows, :] + acc * lax.rsqrt(ms2 + EPS) * gpost_ref[...]
        y_ref[rows, :] = y
        ms3 = jnp.mean(y * y, axis=-1, keepdims=True)
        hn_ref[rows, :] = (y * lax.rsqrt(ms3 + EPS) * gnext_ref[...]).astype(hn_ref.dtype)


def _outproj(oret, osc, ssm_pre, x, w_out, layer, g_ssm, g_post, g_next, tm):
    m = x.shape[0]
    row = lambda width: pl.BlockSpec((tm, width), lambda i: (i, 0))
    vec = lambda width: pl.BlockSpec((1, width), lambda i: (0, 0))
    return pl.pallas_call(
        _outproj_kernel,
        out_shape=(jax.ShapeDtypeStruct((m, D_MODEL), F32), jax.ShapeDtypeStruct((m, D_MODEL), BF16)),
        grid=(m // tm,),
        in_specs=[row(RET_DIM), row(SC_DIM), row(SSM_DIM), row(D_MODEL),
                  pl.BlockSpec((None, D_MIX, D_MODEL), lambda i: (layer, 0, 0), pipeline_mode=pl.Buffered(1)),
                  vec(SSM_DIM), vec(D_MODEL), vec(D_MODEL)],
        out_specs=(row(D_MODEL), row(D_MODEL)),
        compiler_params=_params("parallel"),
        name="outproj",
    )(oret, osc, ssm_pre, x, w_out, g_ssm, g_post, g_next)


def _rope(t, cos2, sin2):
    return t * cos2 + pltpu.roll(t, HEAD_DIM // 2, 1) * sin2


def _head_norm_gate(o, g_row, gate):
    mu = jnp.mean(o, axis=-1, keepdims=True)
    oc = o - mu
    var = jnp.mean(oc * oc, axis=-1, keepdims=True)
    return oc * lax.rsqrt(var + EPS) * g_row * _silu(gate)


def _shift_rows(tail, u, s):
    ext = jnp.concatenate([tail, u], axis=0)
    return ext[SUBLANES - s:SUBLANES - s + u.shape[0]]


def _causal_conv_rows(tail, u, w_ref, b_ref):
    width = w_ref.shape[0]
    out = b_ref[...]
    for j in range(width - 1):
        out = out + _shift_rows(tail, u, width - 1 - j) * w_ref[j:j + 1, :]
    return out + u * w_ref[width - 1:width, :]


def _load_weight_cols(w_refs, w16_ref):
    off = 0
    for w_ref in w_refs:
        n = w_ref.shape[0]
        for r in range(0, n, LANES):
            blk = w_ref[r:min(r + LANES, n), :]
            if blk.shape[0] < LANES:
                blk = jnp.concatenate([blk, jnp.zeros((LANES - blk.shape[0], blk.shape[1]), F32)], axis=0)
            w16_ref[:, off:off + LANES] = jnp.transpose(blk).astype(BF16)
            off += LANES


def _projected_chunks(h_ref, w16_ref, p_even, p_odd, chunk_fn, carry):
    proj_rows = p_even.shape[0]
    assert h_ref.shape[0] % proj_rows == 0 and proj_rows % CHUNK == 0
    steps = h_ref.shape[0] // proj_rows
    per_step = proj_rows // CHUNK

    def project(step, p_ref):
        rows = pl.ds(pl.multiple_of(step * proj_rows, proj_rows), proj_rows)
        p_ref[...] = _dot(h_ref[rows, :], w16_ref[...])

    def mix(step, p_ref, carry):
        for c in range(per_step):
            carry = chunk_fn(p_ref, c * CHUNK, step * per_step + c, carry)
        return carry

    def body(i, carry):
        project(2 * i + 1, p_odd)
        carry = mix(2 * i, p_even, carry)
        project(2 * i + 2, p_even)
        return mix(2 * i + 1, p_odd, carry)

    project(0, p_even)
    full = (steps - 1) // 2
    carry = lax.fori_loop(0, full, body, carry)
    if steps - 2 * full == 2:
        project(2 * full + 1, p_odd)
        carry = mix(2 * full, p_even, carry)
        return mix(2 * full + 1, p_odd, carry)
    return mix(2 * full, p_even, carry)


def _ret_prompt_kernel(h_ref, wq_ref, wk_ref, wv_ref, wg_ref, cos_ref, sin_ref, dmat_ref, qdec_ref, kdec_ref,
                       cdec_ref, gret_ref, o_ref, st_ref, w16_ref, p_even, p_odd, s_scr):
    @pl.when(pl.program_id(1) == 0)
    def _():
        _load_weight_cols((wq_ref, wk_ref, wv_ref, wg_ref), w16_ref)

    width = wq_ref.shape[0]
    heads = width // HEAD_DIM
    s_scr[...] = jnp.zeros_like(s_scr)

    def chunk(p, r0, n, carry):
        rows = pl.ds(pl.multiple_of(n * CHUNK, CHUNK), CHUNK)
        loc = slice(r0, r0 + CHUNK)
        cos2 = cos_ref[rows, :]
        sin2 = sin_ref[rows, :]
        for hh in range(heads):
            sl = slice(hh * HEAD_DIM, (hh + 1) * HEAD_DIM)
            part = lambda i: p[loc, i * width + hh * HEAD_DIM:i * width + (hh + 1) * HEAD_DIM]
            qr = _rope(part(0), cos2, sin2)
            kr = _rope(part(1), cos2, sin2) * (HEAD_DIM ** -0.5)
            qb = qr.astype(BF16)
            vb = part(2).astype(BF16)
            scores = _dot_nt(qb, kr.astype(BF16)) * dmat_ref[hh]
            s_old = s_scr[hh]
            lhs = jnp.concatenate([scores.astype(BF16), (qr * qdec_ref[hh]).astype(BF16)], axis=1)
            o = _dot(lhs, jnp.concatenate([vb, s_old.astype(BF16)], axis=0))
            kd_t = jnp.transpose(kr * kdec_ref[hh]).astype(BF16)
            s_scr[hh] = cdec_ref[hh] * s_old + _dot(kd_t, vb)
            o_ref[rows, sl] = _head_norm_gate(o, gret_ref[:, sl], part(3)).astype(o_ref.dtype)
        return carry

    _projected_chunks(h_ref, w16_ref, p_even, p_odd, chunk, 0)
    st_ref[0] = s_scr[...]


def _ret_prompt(h, w_t, layer, batch, seq, cos2, sin2, tabs, g_ret):
    hp = 4
    width = hp * HEAD_DIM
    wrow = lambda off: pl.BlockSpec((None, width, D_MODEL), lambda c, b, off=off: (layer, off // width + c, 0),
                                    pipeline_mode=pl.Buffered(1))
    tab = pl.BlockSpec((hp, CHUNK, CHUNK), lambda c, b: (c, 0, 0), pipeline_mode=pl.Buffered(1))
    full = pl.BlockSpec((seq, HEAD_DIM), lambda c, b: (0, 0), pipeline_mode=pl.Buffered(1))
    return pl.pallas_call(
        _ret_prompt_kernel,
        out_shape=(jax.ShapeDtypeStruct((batch * seq, RET_DIM), BF16),
                   jax.ShapeDtypeStruct((batch, RET_HEADS, HEAD_DIM, HEAD_DIM), F32)),
        grid=(RET_HEADS // hp, batch),
        in_specs=[pl.BlockSpec((seq, D_MODEL), lambda c, b: (b, 0)),
                  wrow(OFF_Q), wrow(OFF_K), wrow(OFF_V), wrow(OFF_GR), full, full, tab, tab, tab, tab,
                  pl.BlockSpec((1, width), lambda c, b: (0, c))],
        out_specs=(pl.BlockSpec((seq, width), lambda c, b: (b, c)),
                   pl.BlockSpec((1, hp, HEAD_DIM, HEAD_DIM), lambda c, b: (b, c, 0, 0))),
        scratch_shapes=[pltpu.VMEM((D_MODEL, 4 * width), BF16),
                        pltpu.VMEM((PROJ_ROWS, 4 * width), F32), pltpu.VMEM((PROJ_ROWS, 4 * width), F32),
                        pltpu.VMEM((hp, HEAD_DIM, HEAD_DIM), F32)],
        compiler_params=_params("parallel", "arbitrary"),
        name="ret_prompt",
    )(h, w_t, w_t, w_t, w_t, cos2, sin2, *tabs, g_ret)


def _ret_decode_kernel(q_ref, k_ref, v_ref, g_ref, cos_ref, sin_ref, gam_ref, gret_ref, s_ref, acc_ref, o_ref,
                       sn_ref):
    del acc_ref
    bt = q_ref.shape[0]
    assert 2 * bt == ITEMS
    cos2 = cos_ref[...]
    sin2 = sin_ref[...]
    lane = lax.broadcasted_iota(jnp.int32, (LANES, LANES), 1)
    zero = jnp.zeros((bt, HEAD_DIM), F32)
    one = jnp.ones((bt, HEAD_DIM), F32)
    for hp in range(RET_HEADS // 2):
        qr, kr, v, gam, sls = [], [], [], [], []
        for hh in range(2):
            h = 2 * hp + hh
            sl = slice(h * HEAD_DIM, (h + 1) * HEAD_DIM)
            sls.append(sl)
            gam.append(gam_ref[h][0:1, :])
            qr.append(_rope(q_ref[:, sl], cos2, sin2))
            kr.append(_rope(k_ref[:, sl], cos2, sin2) * (HEAD_DIM ** -0.5))
            v.append(v_ref[:, sl])
        k2 = [_bf16_parts(t, 2) for t in kr]
        q3 = [_bf16_parts(t, 3) for t in qr]
        v2 = [_bf16_parts(t, 2) for t in v]
        both = lambda parts, n: [parts[0][n], parts[1][n]]
        a = jnp.concatenate(both(k2, 0) + both(k2, 0) + both(k2, 1) + both(q3, 0) + both(q3, 1) + both(q3, 2)
                            + [zero] * 4, axis=0)
        cols = jnp.transpose(a)
        r_kv = jnp.concatenate(both(v2, 0) + both(v2, 1) + both(v2, 0) + [zero] * 10, axis=0)
        r_q = jnp.concatenate([zero] * 6 + [one] * 6 + [zero] * 4, axis=0)
        r = jnp.concatenate([r_kv, r_q], axis=1).astype(BF16)
        q_s = [[], []]
        for i in range(ITEMS):
            hh, j = divmod(i, bt)
            h = 2 * hp + hh
            out = _dot(jnp.where((lane & (ITEMS - 1)) == i, cols, 0.0).astype(BF16), r)
            s_old = s_ref[j, h]
            sn_ref[j, h] = gam[hh] * s_old + out[:, :HEAD_DIM]
            q_s[hh].append(jnp.sum(s_old * out[:, HEAD_DIM:], axis=0, keepdims=True))
        for hh in range(2):
            qk = jnp.sum(qr[hh] * kr[hh], axis=-1, keepdims=True)
            o = qk * v[hh] + jnp.concatenate(q_s[hh], axis=0) * gam[hh]
            o_ref[:, sls[hh]] = _head_norm_gate(o, gret_ref[:, sls[hh]], g_ref[:, sls[hh]]).astype(o_ref.dtype)


def _ret_decode(proj, states, acc, layer, cos2, sin2, gam_tab, g_ret, bt):
    nb = proj.shape[0]
    col = lambda off: pl.BlockSpec((bt, RET_DIM), lambda i, off=off: (i, off // RET_DIM))
    slab = pl.BlockSpec((None, bt, RET_HEADS, HEAD_DIM, HEAD_DIM), lambda i: (layer, i, 0, 0, 0))
    return pl.pallas_call(
        _ret_decode_kernel,
        out_shape=(jax.ShapeDtypeStruct((nb, RET_DIM), BF16), jax.ShapeDtypeStruct(states.shape, F32)),
        grid=(nb // bt,),
        in_specs=[col(OFF_Q), col(OFF_K), col(OFF_V), col(OFF_GR),
                  pl.BlockSpec((1, HEAD_DIM), lambda i: (0, 0)), pl.BlockSpec((1, HEAD_DIM), lambda i: (0, 0)),
                  pl.BlockSpec((RET_HEADS, SUBLANES, LANES), lambda i: (0, 0, 0)),
                  pl.BlockSpec((1, RET_DIM), lambda i: (0, 0)), slab, pl.BlockSpec(memory_space=pl.ANY)],
        out_specs=(pl.BlockSpec((bt, RET_DIM), lambda i: (i, 0)), slab),
        input_output_aliases={9: 1},
        compiler_params=_params("parallel"),
        name="ret_decode",
    )(proj, proj, proj, proj, cos2, sin2, gam_tab, g_ret, states, acc)


def _sc_prompt_kernel(h_ref, wbg_ref, wcg_ref, wsh_ref, wgs_ref, w_ref, b_ref, o_ref, st_ref, w16_ref, p_even,
                      p_odd):
    @pl.when(pl.program_id(1) == 0)
    def _():
        _load_weight_cols((wbg_ref, wcg_ref, wsh_ref, wgs_ref), w16_ref)

    width = wbg_ref.shape[0]

    def chunk(p, r0, n, tail):
        rows = pl.ds(pl.multiple_of(n * CHUNK, CHUNK), CHUNK)
        part = lambda i: p[r0:r0 + CHUNK, i * width:(i + 1) * width]
        u = part(1) * part(2)
        conv = _causal_conv_rows(tail, u, w_ref, b_ref)
        o_ref[rows, :] = (part(0) * conv * _silu(part(3))).astype(o_ref.dtype)
        return u[CHUNK - SUBLANES:, :]

    tail = _projected_chunks(h_ref, w16_ref, p_even, p_odd, chunk, jnp.zeros((SUBLANES, width), F32))
    st_ref[0] = tail[SUBLANES - (SC_WIDTH - 1):, :]


def _sc_prompt(h, w_t, layer, batch, seq, w, b):
    width = 512
    proj_rows = min(2 * PROJ_ROWS, seq)
    wrow = lambda off: pl.BlockSpec((None, width, D_MODEL), lambda c, bi, off=off: (layer, off // width + c, 0),
                                    pipeline_mode=pl.Buffered(1))
    return pl.pallas_call(
        _sc_prompt_kernel,
        out_shape=(jax.ShapeDtypeStruct((batch * seq, SC_DIM), BF16),
                   jax.ShapeDtypeStruct((batch, SC_WIDTH - 1, SC_DIM), F32)),
        grid=(SC_DIM // width, batch),
        in_specs=[pl.BlockSpec((seq, D_MODEL), lambda c, bi: (bi, 0)),
                  wrow(OFF_BG), wrow(OFF_CG), wrow(OFF_SH), wrow(OFF_GS),
                  pl.BlockSpec((SC_WIDTH, width), lambda c, bi: (0, c)),
                  pl.BlockSpec((1, width), lambda c, bi: (0, c))],
        out_specs=(pl.BlockSpec((seq, width), lambda c, bi: (bi, c)),
                   pl.BlockSpec((1, SC_WIDTH - 1, width), lambda c, bi: (bi, 0, c))),
        scratch_shapes=[pltpu.VMEM((D_MODEL, 4 * width), BF16),
                        pltpu.VMEM((proj_rows, 4 * width), F32), pltpu.VMEM((proj_rows, 4 * width), F32)],
        compiler_params=_params("parallel", "arbitrary"),
        name="sc_prompt",
    )(h, w_t, w_t, w_t, w_t, w, b)


def _sc_decode_kernel(bg_ref, cg_ref, sh_ref, gs_ref, st_ref, w_ref, b_ref, o_ref, stn_ref):
    u = cg_ref[...] * sh_ref[...]
    r0 = st_ref[:, 0:SC_DIM]
    r1 = st_ref[:, SC_DIM:]
    conv = b_ref[...] + r0 * w_ref[0:1, :] + r1 * w_ref[1:2, :] + u * w_ref[2:3, :]
    o_ref[...] = (bg_ref[...] * conv * _silu(gs_ref[...])).astype(o_ref.dtype)
    stn_ref[:, 0:SC_DIM] = r1
    stn_ref[:, SC_DIM:] = u


def _sc_decode(proj, state2d, w, b):
    nb = proj.shape[0]
    col = lambda off: pl.BlockSpec((nb, SC_DIM), lambda i, off=off: (0, off // SC_DIM))
    whole = lambda shape: pl.BlockSpec(shape, lambda i: (0,) * len(shape))
    return pl.pallas_call(
        _sc_decode_kernel,
        out_shape=(jax.ShapeDtypeStruct((nb, SC_DIM), BF16), jax.ShapeDtypeStruct(state2d.shape, F32)),
        grid=(1,),
        in_specs=[col(OFF_BG), col(OFF_CG), col(OFF_SH), col(OFF_GS), whole(state2d.shape),
                  whole((SC_WIDTH, SC_DIM)), whole((1, SC_DIM))],
        out_specs=(whole((nb, SC_DIM)), whole(state2d.shape)),
        compiler_params=_params("arbitrary"),
        name="sc_decode",
    )(proj, proj, proj, proj, state2d, w, b)


def _lane_cumsum(a, lane):
    s = 1
    while s < CHUNK:
        a = a + jnp.where(lane >= s, pltpu.roll(a, s, 1), 0.0)
        s *= 2
    return a


def _ssd_prompt_kernel(h_ref, dtr_ref, wz_ref, wx_ref, wb_ref, wc_ref, dtb_ref, alog_ref, dsk_ref,
                       cwx_ref, cwb_ref, cwc_ref, cbx_ref, cbb_ref, cbc_ref,
                       y_ref, stx_ref, stb_ref, stc_ref, sst_ref, w16_ref, p_even, p_odd, s_scr, dt_scr):
    @pl.when(pl.program_id(1) == 0)
    def _():
        _load_weight_cols((wz_ref, wx_ref, wb_ref, wc_ref), w16_ref)

    gw = wz_ref.shape[0]
    n_st = wb_ref.shape[0]
    off_x, off_b, off_c = gw, 2 * gw, 2 * gw + n_st
    first_head = pl.multiple_of(pl.program_id(0) * HEADS_PER_GROUP, HEADS_PER_GROUP)
    pad_lanes = jnp.zeros((CHUNK, LANES - SSM_HEADS), F32)
    row = lax.broadcasted_iota(jnp.int32, (CHUNK, CHUNK), 0)
    lane = lax.broadcasted_iota(jnp.int32, (CHUNK, CHUNK), 1)
    lane8 = lax.broadcasted_iota(jnp.int32, (HEADS_PER_GROUP, CHUNK), 1)
    causal = row >= lane
    low_lanes = lane < SSM_HEAD_DIM
    low_rows = row < SSM_HEAD_DIM
    a_neg = -jnp.exp(alog_ref[0])
    dt_bias = dtb_ref[0]
    pad_heads = jnp.zeros((CHUNK - HEADS_PER_GROUP, CHUNK), F32)
    s_scr[...] = jnp.zeros_like(s_scr)

    def chunk(p, r0, n, tails):
        tx, tb, tc = tails
        rows = pl.ds(pl.multiple_of(n * CHUNK, CHUNK), CHUNK)
        loc = slice(r0, r0 + CHUNK)
        x_raw = p[loc, off_x:off_x + gw]
        b_raw = p[loc, off_b:off_b + n_st]
        c_raw = p[loc, off_c:off_c + n_st]
        xc = _silu(_causal_conv_rows(tx, x_raw, cwx_ref, cbx_ref))
        bb = _silu(_causal_conv_rows(tb, b_raw, cwb_ref, cbb_ref)).astype(BF16)
        cb16 = _silu(_causal_conv_rows(tc, c_raw, cwc_ref, cbc_ref)).astype(BF16)
        dt_scr[...] = jnp.transpose(jnp.concatenate([dtr_ref[rows, :], pad_lanes], axis=1))
        dt_t = _softplus(dt_scr[pl.ds(first_head, HEADS_PER_GROUP), :] + dt_bias)
        acum_t = _lane_cumsum(dt_t * a_neg, lane8)
        acum = jnp.transpose(jnp.concatenate([acum_t, pad_heads], axis=0))
        cb = _dot_nt(cb16, bb)
        y_intra, e_sel, dec_sel, xw, s_old = [], [], [], [], []
        for pp in range(PAIRS_PER_GROUP):
            x_pair = xc[:, pp * PAIR:(pp + 1) * PAIR]
            m, e_col, w_row, dec = [], [], [], []
            for hh in range(2):
                c = 2 * pp + hh
                a_col = jnp.broadcast_to(acum[:, c:c + 1], (CHUNK, CHUNK))
                a_row = acum_t[c:c + 1, :]
                dt_row = dt_t[c:c + 1, :]
                lmat = jnp.exp(jnp.where(causal, a_col - a_row, -jnp.inf))
                m.append((cb * lmat * dt_row).astype(BF16))
                e_col.append(jnp.exp(a_col))
                last = a_col[CHUNK - 1:CHUNK, :]
                w_row.append(jnp.exp(last - a_row) * dt_row)
                dec.append(jnp.exp(last))
            x_stack = jnp.concatenate([jnp.where(low_lanes, x_pair, 0.0), jnp.where(low_lanes, 0.0, x_pair)], axis=0)
            y_intra.append(_dot(jnp.concatenate(m, axis=1), x_stack.astype(BF16)))
            e_sel.append(jnp.where(low_lanes, e_col[0], e_col[1]))
            dec_sel.append(jnp.where(low_rows, dec[0], dec[1]))
            xw.append((jnp.transpose(x_pair) * jnp.where(low_rows, w_row[0], w_row[1])).astype(BF16))
            s_old.append(s_scr[pp])
        s_upd = _dot(jnp.concatenate(xw, axis=0), bb)
        for q in range(PAIRS_PER_GROUP // 2):
            s_two = jnp.concatenate([s_old[2 * q], s_old[2 * q + 1]], axis=0).astype(BF16)
            y_two = _dot_nt(cb16, s_two)
            for r in range(2):
                pp = 2 * q + r
                sl = slice(pp * PAIR, (pp + 1) * PAIR)
                y = y_intra[pp] + y_two[:, r * PAIR:(r + 1) * PAIR] * e_sel[pp] + dsk_ref[0][:, sl] * xc[:, sl]
                y_ref[rows, sl] = y * _silu(p[loc, sl])
                s_scr[pp] = dec_sel[pp] * s_old[pp] + s_upd[pp * PAIR:(pp + 1) * PAIR, :]
        cut = CHUNK - SUBLANES
        return x_raw[cut:, :], b_raw[cut:, :], c_raw[cut:, :]

    zeros = lambda width: jnp.zeros((SUBLANES, width), F32)
    tx, tb, tc = _projected_chunks(h_ref, w16_ref, p_even, p_odd, chunk, (zeros(gw), zeros(n_st), zeros(n_st)))
    keep = SUBLANES - (SSM_CONV - 1)
    stx_ref[0] = tx[keep:, :]
    stb_ref[0] = tb[keep:, :]
    stc_ref[0] = tc[keep:, :]
    for pp in range(PAIRS_PER_GROUP):
        for hh in range(2):
            sst_ref[0, 2 * pp + hh] = s_scr[pp, hh * SSM_HEAD_DIM:(hh + 1) * SSM_HEAD_DIM, :]


def _dt_kernel(x_ref, w_ref, o_ref):
    o_ref[...] = _dot_nt(x_ref[...], w_ref[...].astype(BF16))


def _dt_prompt(h, w_t, layer, tm):
    m, k = h.shape
    return pl.pallas_call(
        _dt_kernel,
        out_shape=jax.ShapeDtypeStruct((m, SSM_HEADS), F32),
        grid=(m // tm,),
        in_specs=[pl.BlockSpec((tm, k), lambda i: (i, 0)),
                  pl.BlockSpec((None, SSM_HEADS, k), lambda i: (layer, OFF_DT // SSM_HEADS, 0))],
        out_specs=pl.BlockSpec((tm, SSM_HEADS), lambda i: (i, 0)),
        compiler_params=_params("parallel"),
        name="dt_prompt",
    )(h, w_t)


def _ssd_prompt(h, dt_raw, w_t, layer, batch, seq, dtb, alog, dsk, conv_w, conv_b):
    gw = SSM_DIM // SSM_GROUPS
    n = SSM_STATE
    wrow = lambda off, width: pl.BlockSpec((None, width, D_MODEL),
                                           lambda g, b, off=off, width=width: (layer, off // width + g, 0),
                                           pipeline_mode=pl.Buffered(1))
    grp = lambda rows, width: pl.BlockSpec((1, rows, width), lambda g, b: (g, 0, 0))
    cw = lambda off, width: pl.BlockSpec((SSM_CONV, width), lambda g, b, off=off, width=width: (0, off // width + g))
    cbias = lambda off, width: pl.BlockSpec((1, width), lambda g, b, off=off, width=width: (0, off // width + g))
    st = lambda width: pl.BlockSpec((1, SSM_CONV - 1, width), lambda g, b: (b, 0, g))
    n_proj = 2 * gw + 2 * n
    return pl.pallas_call(
        _ssd_prompt_kernel,
        out_shape=(jax.ShapeDtypeStruct((batch * seq, SSM_DIM), F32),
                   jax.ShapeDtypeStruct((batch, SSM_CONV - 1, SSM_DIM), F32),
                   jax.ShapeDtypeStruct((batch, SSM_CONV - 1, SSM_GROUPS * n), F32),
                   jax.ShapeDtypeStruct((batch, SSM_CONV - 1, SSM_GROUPS * n), F32),
                   jax.ShapeDtypeStruct((batch, SSM_HEADS, SSM_HEAD_DIM, n), F32)),
        grid=(SSM_GROUPS, batch),
        in_specs=[pl.BlockSpec((seq, D_MODEL), lambda g, b: (b, 0)),
                  pl.BlockSpec((seq, SSM_HEADS), lambda g, b: (b, 0)),
                  wrow(OFF_Z, gw), wrow(OFF_X, gw), wrow(OFF_B, n), wrow(OFF_C, n),
                  grp(HEADS_PER_GROUP, CHUNK), grp(HEADS_PER_GROUP, CHUNK), grp(1, gw),
                  cw(0, gw), cw(SSM_DIM, n), cw(SSM_DIM + SSM_GROUPS * n, n),
                  cbias(0, gw), cbias(SSM_DIM, n), cbias(SSM_DIM + SSM_GROUPS * n, n)],
        out_specs=(pl.BlockSpec((seq, gw), lambda g, b: (b, g)), st(gw), st(n), st(n),
                   pl.BlockSpec((1, HEADS_PER_GROUP, SSM_HEAD_DIM, n), lambda g, b: (b, g, 0, 0))),
        scratch_shapes=[pltpu.VMEM((D_MODEL, n_proj), BF16),
                        pltpu.VMEM((PROJ_ROWS, n_proj), F32), pltpu.VMEM((PROJ_ROWS, n_proj), F32),
                        pltpu.VMEM((PAIRS_PER_GROUP, PAIR, n), F32), pltpu.VMEM((LANES, CHUNK), F32)],
        compiler_params=_params("parallel", "arbitrary"),
        name="ssd_prompt",
    )(h, dt_raw, w_t, w_t, w_t, w_t, dtb, alog, dsk, conv_w, conv_w, conv_w, conv_b, conv_b, conv_b)


def _ssd_decode_kernel(z_ref, x_ref, bc_ref, dtr_ref, cst_ref, s_ref, cw_ref, cb_ref, dtb_ref, alog_ref, dsk_ref,
                       acc_ref, y_ref, cstn_ref, sn_ref):
    del acc_ref
    bt = z_ref.shape[0]
    xr = SSM_DIM // LANES
    gr = SSM_GROUPS * SSM_STATE // LANES
    u = jnp.concatenate([x_ref[...], bc_ref[...]], axis=1)
    conv = cb_ref[...]
    for j in range(SSM_CONV - 1):
        conv = conv + cst_ref[:, j] * cw_ref[j]
        cstn_ref[:, j] = u if j == SSM_CONV - 2 else cst_ref[:, j + 1]
    act = _silu(conv + u * cw_ref[SSM_CONV - 1])
    xs = act[:, 0:xr]
    bm = act[:, xr:xr + gr]
    cm = act[:, xr + gr:]
    dt = _softplus(dtr_ref[...] + dtb_ref[...])
    ea = jnp.exp(dt * -jnp.exp(alog_ref[...]))
    xdt = xs * dt
    assert xr == ITEMS == N_PAIRS
    lane = lax.broadcasted_iota(jnp.int32, (LANES, LANES), 1)
    row16 = lax.broadcasted_iota(jnp.int32, (xr, LANES), 0)
    zeros = lambda n: jnp.zeros((n, LANES), F32)
    r_ea = jnp.concatenate([zeros(3 * ITEMS), jnp.ones((3 * ITEMS, LANES), F32), zeros(2 * ITEMS)], axis=0)
    for j in range(bt):
        cbt = zeros(xr)
        b_rows = zeros(xr)
        for g in range(SSM_GROUPS):
            in_group = row16 // PAIRS_PER_GROUP == g
            cb_g = jnp.sum(cm[j, g:g + 1, :] * bm[j, g:g + 1, :], axis=-1, keepdims=True)
            cbt = jnp.where(in_group, cb_g, cbt)
            b_rows = jnp.where(in_group, bm[j, g:g + 1, :], b_rows)
        xd2 = _bf16_parts(xdt[j], 2)
        ea3 = _bf16_parts(ea[j], 3)
        b2 = _bf16_parts(b_rows, 2)
        a = jnp.concatenate([xd2[0], xd2[0], xd2[1]] + ea3 + [zeros(2 * ITEMS)], axis=0)
        cols = jnp.transpose(a)
        r_xb = jnp.concatenate([b2[0], b2[1], b2[0], zeros(5 * ITEMS)], axis=0)
        r = jnp.concatenate([r_xb, r_ea], axis=1).astype(BF16)
        y_t = jnp.zeros((LANES, LANES), F32)
        for pp in range(N_PAIRS):
            g = pp // PAIRS_PER_GROUP
            out = _dot(jnp.where((lane & (ITEMS - 1)) == pp, cols, 0.0).astype(BF16), r)
            s_old = s_ref[j, 2 * pp:2 * pp + 2].reshape(PAIR, SSM_STATE)
            y_col = jnp.sum(s_old * cm[j, g:g + 1, :], axis=-1, keepdims=True)
            y_t = jnp.where(lane == pp, y_col, y_t)
            s_new = out[:, SSM_STATE:] * s_old + out[:, :SSM_STATE]
            sn_ref[j, 2 * pp:2 * pp + 2] = s_new.reshape(2, SSM_HEAD_DIM, SSM_STATE)
        y_inter = jnp.transpose(y_t)[0:xr, :]
        y = cbt * xdt[j] + y_inter * ea[j] + dsk_ref[0] * xs[j]
        y_ref[j] = y * _silu(z_ref[j])


def _ssd_decode(proj3, dt_raw3, cstates, states, acc, layer, conv_w3, conv_b3, dtb3, alog3, dsk3, bt):
    nb = proj3.shape[0]
    xr = SSM_DIM // LANES
    cr = SSM_CONV_DIM // LANES
    whole = lambda shape: pl.BlockSpec(shape, lambda i: (0,) * len(shape))
    cst = pl.BlockSpec((bt, SSM_CONV - 1, cr, LANES), lambda i: (i, 0, 0, 0))
    st_in = pl.BlockSpec((None, bt, SSM_HEADS, SSM_HEAD_DIM, SSM_STATE), lambda i: (layer, i, 0, 0, 0))
    cst_in = pl.BlockSpec((None, bt, SSM_CONV - 1, cr, LANES), lambda i: (layer, i, 0, 0, 0))
    return pl.pallas_call(
        _ssd_decode_kernel,
        out_shape=(jax.ShapeDtypeStruct((nb, xr, LANES), F32), jax.ShapeDtypeStruct(cstates.shape[1:], F32),
                   jax.ShapeDtypeStruct(states.shape, F32)),
        grid=(nb // bt,),
        in_specs=[pl.BlockSpec((bt, xr, LANES), lambda i: (i, OFF_Z // SSM_DIM, 0)),
                  pl.BlockSpec((bt, xr, LANES), lambda i: (i, OFF_X // SSM_DIM, 0)),
                  pl.BlockSpec((bt, SUBLANES, LANES), lambda i: (i, OFF_B // (SUBLANES * LANES), 0)),
                  pl.BlockSpec((bt, xr, LANES), lambda i: (i, 0, 0)),
                  cst_in, st_in, whole((SSM_CONV, cr, LANES)), whole((1, cr, LANES)),
                  whole((1, xr, LANES)), whole((1, xr, LANES)), whole((1, xr, LANES)),
                  pl.BlockSpec(memory_space=pl.ANY)],
        out_specs=(pl.BlockSpec((bt, xr, LANES), lambda i: (i, 0, 0)), cst, st_in),
        input_output_aliases={11: 2},
        compiler_params=_params("parallel"),
        name="ssd_decode",
    )(proj3, proj3, proj3, dt_raw3, cstates, states, conv_w3, conv_b3, dtb3, alog3, dsk3, acc)


def _rope_tables(pos):
    half = HEAD_DIM // 2
    inv = ROPE_BASE ** (-jnp.arange(half, dtype=F32) / half)
    ang = pos.astype(F32)[:, None] * inv[None, :]
    cos, sin = jnp.cos(ang), jnp.sin(ang)
    return jnp.concatenate([cos, cos], axis=-1), jnp.concatenate([-sin, sin], axis=-1)


def _retention_tables():
    lg = jnp.log(1.0 - 2.0 ** (-5.0 - jnp.arange(RET_HEADS, dtype=F32)))
    i = jnp.arange(CHUNK, dtype=F32)
    diff = i[:, None] - i[None, :]
    dmat = jnp.exp(jnp.where((diff >= 0)[None], diff[None] * lg[:, None, None], -jnp.inf))
    full = lambda t: jnp.broadcast_to(t, (RET_HEADS, CHUNK, CHUNK))
    q_dec = full(jnp.exp((i + 1.0)[None, :, None] * lg[:, None, None]))
    k_dec = full(jnp.exp((CHUNK - 1.0 - i)[None, :, None] * lg[:, None, None]))
    c_dec = full(jnp.exp(CHUNK * lg)[:, None, None])
    gam = jnp.broadcast_to(jnp.exp(lg)[:, None, None], (RET_HEADS, SUBLANES, LANES))
    return (dmat, q_dec, k_dec, c_dec), gam


def _per_group(v):
    v = v.astype(F32).reshape(SSM_GROUPS, HEADS_PER_GROUP, 1)
    return jnp.broadcast_to(v, (SSM_GROUPS, HEADS_PER_GROUP, CHUNK))


def _per_lane(v):
    return jnp.repeat(v.astype(F32), SSM_HEAD_DIM)[None, :]


def kernel(x_prompt, x_sample, state_ret, state_sconv, state_ssm_conv, state_ssm, w_in, w_out, norm_pre, norm_post,
           ret_norm, sc_conv_w, sc_conv_b, ssm_conv_w, ssm_conv_b, ssm_dt_bias, ssm_a_log, ssm_d, ssm_norm):
    batch, seq, _ = x_prompt.shape
    nb = x_sample.shape[0]
    depth = w_in.shape[0]
    mp = batch * seq
    xr = SSM_DIM // LANES
    cr = SSM_CONV_DIM // LANES

    cos_p, sin_p = _rope_tables(jnp.arange(seq))
    cos_s, sin_s = _rope_tables(PAST_LEN + jnp.arange(1))
    ret_tabs, gam_tab = _retention_tables()

    w_out16 = w_out.astype(BF16)
    w_t = jnp.swapaxes(w_in, 1, 2)
    w_dt_lane = jnp.repeat(w_t[:, OFF_DT:, :], SSM_HEAD_DIM, axis=1).astype(BF16)

    tm_o = min(2 * OUT_SUB_ROWS, mp)
    xp = x_prompt.reshape(mp, D_MODEL)
    xs = x_sample.reshape(nb, D_MODEL)
    hp = _prenorm(xp, norm_pre[0][None, :], tm_o)
    hs = _prenorm(xs, norm_pre[0][None, :], nb)

    outs = [[] for _ in range(6)]
    ret_acc = jnp.zeros(state_ret.shape, F32)
    ssm_acc = jnp.zeros(state_ssm.shape, F32)
    for l in range(depth):
        g_next = norm_pre[(l + 1) % depth][None, :]
        dtb_g, alog_g = _per_group(ssm_dt_bias[l]), _per_group(ssm_a_log[l])
        dsk_lane = _per_lane(ssm_d[l])
        dsk_g = dsk_lane.reshape(SSM_GROUPS, 1, SSM_DIM // SSM_GROUPS)
        conv_b = ssm_conv_b[l][None, :]

        o_ret, r_new = _ret_prompt(hp, w_t, l, batch, seq, cos_p, sin_p, ret_tabs, ret_norm[l][None, :])
        o_sc, c_new = _sc_prompt(hp, w_t, l, batch, seq, sc_conv_w[l], sc_conv_b[l][None, :])
        dt_raw = _dt_prompt(hp, w_t, l, min(1024, mp))
        ssm_pre, cx, cb_, cc, s_new = _ssd_prompt(hp, dt_raw, w_t, l, batch, seq, dtb_g, alog_g, dsk_g, ssm_conv_w[l],
                                                  conv_b)
        xp, hp = _outproj(o_ret, o_sc, ssm_pre, xp, w_out16, l, ssm_norm[l][None, :], norm_post[l][None, :],
                          g_next, tm_o)
        outs[0].append(r_new)
        outs[1].append(c_new)
        outs[2].append(jnp.concatenate([cx, cb_, cc], axis=-1))
        outs[3].append(s_new)

        proj_s = _matmul_nt(hs, w_t, l, N_MAIN, 1024, "inproj_decode")
        dt_s = _matmul_nt(hs, w_dt_lane, l, SSM_DIM, SSM_DIM, "dtproj_decode")
        o_ret, ret_acc = _ret_decode(proj_s, state_ret, ret_acc, l, cos_s, sin_s, gam_tab, ret_norm[l][None, :],
                                     SUBLANES)
        o_sc, c_new = _sc_decode(proj_s, state_sconv[l].reshape(nb, (SC_WIDTH - 1) * SC_DIM), sc_conv_w[l],
                                 sc_conv_b[l][None, :])
        y_s, cs_new, ssm_acc = _ssd_decode(
            proj_s.reshape(nb, N_MAIN // LANES, LANES), dt_s.reshape(nb, xr, LANES),
            state_ssm_conv.reshape(depth, nb, SSM_CONV - 1, cr, LANES), state_ssm, ssm_acc, l,
            ssm_conv_w[l].reshape(SSM_CONV, cr, LANES), conv_b.reshape(1, cr, LANES),
            _per_lane(ssm_dt_bias[l]).reshape(1, xr, LANES), _per_lane(ssm_a_log[l]).reshape(1, xr, LANES),
            dsk_lane.reshape(1, xr, LANES), 4)
        xs, hs = _outproj(o_ret, o_sc, y_s.reshape(nb, SSM_DIM), xs, w_out16, l, ssm_norm[l][None, :],
                          norm_post[l][None, :], g_next, nb)
        outs[4].append(c_new.reshape(nb, SC_WIDTH - 1, SC_DIM))
        outs[5].append(cs_new.reshape(nb, SSM_CONV - 1, SSM_CONV_DIM))

    stacked = [jnp.stack(o) for o in outs]
    return (xp.reshape(batch, seq, D_MODEL), xs.reshape(nb, 1, D_MODEL), *stacked[:4],
            ret_acc, stacked[4], stacked[5], ssm_acc)
```

```python
import functools

import jax
import jax.numpy as jnp
import numpy as np
from jax import lax
from jax.experimental import pallas as pl
from jax.experimental.pallas import tpu as pltpu

F32 = jnp.float32
BF16 = jnp.bfloat16

D_MODEL = 2048
D_MIX = 2 * D_MODEL
RET_HEADS = 8
RET_DIM = 1024
HEAD_DIM = 128
SC_DIM = 1024
SC_WIDTH = 3
SSM_DIM = 2048
SSM_HEAD_DIM = 64
SSM_HEADS = 32
SSM_GROUPS = 4
SSM_STATE = 128
SSM_CONV = 4
SSM_CONV_DIM = SSM_DIM + 2 * SSM_GROUPS * SSM_STATE
CHUNK = 128
PROJ_ROWS = 2 * CHUNK
OUT_SUB_ROWS = 256
ROPE_BASE = 10000.0
EPS = 1e-6
PAST_LEN = 16384

OFF_Q, OFF_K, OFF_V, OFF_GR = 0, 1024, 2048, 3072
OFF_BG, OFF_CG, OFF_SH, OFF_GS = 4096, 5120, 6144, 7168
OFF_Z, OFF_X, OFF_B, OFF_C, OFF_DT = 8192, 10240, 12288, 12800, 13312
N_MAIN = OFF_DT

LANES = 128
SUBLANES = 8
V7X_VMEM_LIMIT = 60 * 1024 * 1024
PAIR = 2 * SSM_HEAD_DIM
HEADS_PER_GROUP = SSM_HEADS // SSM_GROUPS
PAIRS_PER_GROUP = HEADS_PER_GROUP // 2
N_PAIRS = SSM_HEADS // 2
ITEMS = 16


def _params(*sem):
    return pltpu.CompilerParams(dimension_semantics=sem, vmem_limit_bytes=V7X_VMEM_LIMIT)


def _silu(x):
    return x * jax.nn.sigmoid(x)


def _softplus(x):
    return jnp.maximum(x, 0.0) + jnp.log1p(jnp.exp(-jnp.abs(x)))


def _dot(a, b):
    return jnp.dot(a, b, preferred_element_type=F32)


def _bf16_parts(x, n):
    parts = []
    for _ in range(n):
        p = x.astype(BF16).astype(F32)
        parts.append(p)
        x = x - p
    return parts


def _dot_nt(a, b):
    return lax.dot_general(a, b, (((1,), (1,)), ((), ())), preferred_element_type=F32)


def _prenorm_kernel(x_ref, g_ref, o_ref):
    x = x_ref[...]
    ms = jnp.mean(x * x, axis=-1, keepdims=True)
    o_ref[...] = (x * lax.rsqrt(ms + EPS) * g_ref[...]).astype(o_ref.dtype)


def _prenorm(x, g, tm):
    m, d = x.shape
    return pl.pallas_call(
        _prenorm_kernel,
        out_shape=jax.ShapeDtypeStruct((m, d), BF16),
        grid=(m // tm,),
        in_specs=[pl.BlockSpec((tm, d), lambda i: (i, 0)), pl.BlockSpec((1, d), lambda i: (0, 0))],
        out_specs=pl.BlockSpec((tm, d), lambda i: (i, 0)),
        compiler_params=_params("parallel"),
        name="prenorm",
    )(x, g)


def _mm_nt_kernel(x_ref, w_ref, o_ref):
    o_ref[...] = _dot_nt(x_ref[...], w_ref[...].astype(BF16))


def _matmul_nt(x, w_t, layer, n_cols, tn, name):
    m, k = x.shape
    return pl.pallas_call(
        _mm_nt_kernel,
        out_shape=jax.ShapeDtypeStruct((m, n_cols), F32),
        grid=(n_cols // tn,),
        in_specs=[pl.BlockSpec((m, k), lambda j: (0, 0)),
                  pl.BlockSpec((None, tn, k), lambda j: (layer, j, 0))],
        out_specs=pl.BlockSpec((m, tn), lambda j: (0, j)),
        compiler_params=_params("parallel"),
        name=name,
    )(x, w_t)


def _outproj_kernel(oret_ref, osc_ref, ssm_ref, x_ref, w_ref, gssm_ref, gpost_ref, gnext_ref, y_ref, hn_ref):
    tm = x_ref.shape[0]
    sub = min(tm, OUT_SUB_ROWS)
    for r0 in range(0, tm, sub):
        rows = slice(r0, r0 + sub)
        ypre = ssm_ref[rows, :]
        ms = jnp.mean(ypre * ypre, axis=-1, keepdims=True)
        ossm = (ypre * lax.rsqrt(ms + EPS) * gssm_ref[...]).astype(BF16)
        acc = _dot(oret_ref[rows, :], w_ref[0:RET_DIM, :])
        acc = acc + _dot(osc_ref[rows, :], w_ref[RET_DIM:RET_DIM + SC_DIM, :])
        acc = acc + _dot(ossm, w_ref[RET_DIM + SC_DIM:, :])
        ms2 = jnp.mean(acc * acc, axis=-1, keepdims=True)
        y = x_ref[rows, :] + acc * lax.rsqrt(ms2 + EPS) * gpost_ref[...]
        y_ref[rows, :] = y
        ms3 = jnp.mean(y * y, axis=-1, keepdims=True)
        hn_ref[rows, :] = (y * lax.rsqrt(ms3 + EPS) * gnext_ref[...]).astype(hn_ref.dtype)


def _outproj(oret, osc, ssm_pre, x, w_out, layer, g_ssm, g_post, g_next, tm):
    m = x.shape[0]
    row = lambda width: pl.BlockSpec((tm, width), lambda i: (i, 0))
    vec = lambda width: pl.BlockSpec((1, width), lambda i: (0, 0))
    return pl.pallas_call(
        _outproj_kernel,
        out_shape=(jax.ShapeDtypeStruct((m, D_MODEL), F32), jax.ShapeDtypeStruct((m, D_MODEL), BF16)),
        grid=(m // tm,),
        in_specs=[row(RET_DIM), row(SC_DIM), row(SSM_DIM), row(D_MODEL),
                  pl.BlockSpec((None, D_MIX, D_MODEL), lambda i: (layer, 0, 0), pipeline_mode=pl.Buffered(1)),
                  vec(SSM_DIM), vec(D_MODEL), vec(D_MODEL)],
        out_specs=(row(D_MODEL), row(D_MODEL)),
        compiler_params=_params("parallel"),
        name="outproj",
    )(oret, osc, ssm_pre, x, w_out, g_ssm, g_post, g_next)


def _rope(t, cos2, sin2):
    return t * cos2 + pltpu.roll(t, HEAD_DIM // 2, 1) * sin2


def _head_norm_gate(o, g_row, gate):
    mu = jnp.mean(o, axis=-1, keepdims=True)
    oc = o - mu
    var = jnp.mean(oc * oc, axis=-1, keepdims=True)
    return oc * lax.rsqrt(var + EPS) * g_row * _silu(gate)


def _shift_rows(tail, u, s):
    ext = jnp.concatenate([tail, u], axis=0)
    return ext[SUBLANES - s:SUBLANES - s + u.shape[0]]


def _causal_conv_rows(tail, u, w_ref, b_ref):
    width = w_ref.shape[0]
    out = b_ref[...]
    for j in range(width - 1):
        out = out + _shift_rows(tail, u, width - 1 - j) * w_ref[j:j + 1, :]
    return out + u * w_ref[width - 1:width, :]


def _load_weight_cols(w_refs, w16_ref):
    off = 0
    for w_ref in w_refs:
        n = w_ref.shape[0]
        for r in range(0, n, LANES):
            blk = w_ref[r:min(r + LANES, n), :]
            if blk.shape[0] < LANES:
                blk = jnp.concatenate([blk, jnp.zeros((LANES - blk.shape[0], blk.shape[1]), F32)], axis=0)
            w16_ref[:, off:off + LANES] = jnp.transpose(blk).astype(BF16)
            off += LANES


def _projected_chunks(h_ref, w16_ref, p_even, p_odd, chunk_fn, carry):
    proj_rows = p_even.shape[0]
    assert h_ref.shape[0] % proj_rows == 0 and proj_rows % CHUNK == 0
    steps = h_ref.shape[0] // proj_rows
    per_step = proj_rows // CHUNK

    def project(step, p_ref):
        rows = pl.ds(pl.multiple_of(step * proj_rows, proj_rows), proj_rows)
        p_ref[...] = _dot(h_ref[rows, :], w16_ref[...])

    def mix(step, p_ref, carry):
        for c in range(per_step):
            carry = chunk_fn(p_ref, c * CHUNK, step * per_step + c, carry)
        return carry

    def body(i, carry):
        project(2 * i + 1, p_odd)
        carry = mix(2 * i, p_even, carry)
        project(2 * i + 2, p_even)
        return mix(2 * i + 1, p_odd, carry)

    project(0, p_even)
    full = (steps - 1) // 2
    carry = lax.fori_loop(0, full, body, carry)
    if steps - 2 * full == 2:
        project(2 * full + 1, p_odd)
        carry = mix(2 * full, p_even, carry)
        return mix(2 * full + 1, p_odd, carry)
    return mix(2 * full, p_even, carry)


def _ret_prompt_kernel(h_ref, wq_ref, wk_ref, wv_ref, wg_ref, cos_ref, sin_ref, dmat_ref, qdec_ref, kdec_ref,
                       cdec_ref, gret_ref, o_ref, st_ref, w16_ref, p_even, p_odd, s_scr):
    @pl.when(pl.program_id(1) == 0)
    def _():
        _load_weight_cols((wq_ref, wk_ref, wv_ref, wg_ref), w16_ref)

    width = wq_ref.shape[0]
    heads = width // HEAD_DIM
    s_scr[...] = jnp.zeros_like(s_scr)

    def chunk(p, r0, n, carry):
        rows = pl.ds(pl.multiple_of(n * CHUNK, CHUNK), CHUNK)
        loc = slice(r0, r0 + CHUNK)
        cos2 = cos_ref[rows, :]
        sin2 = sin_ref[rows, :]
        for hh in range(heads):
            sl = slice(hh * HEAD_DIM, (hh + 1) * HEAD_DIM)
            part = lambda i: p[loc, i * width + hh * HEAD_DIM:i * width + (hh + 1) * HEAD_DIM]
            qr = _rope(part(0), cos2, sin2)
            kr = _rope(part(1), cos2, sin2) * (HEAD_DIM ** -0.5)
            qb = qr.astype(BF16)
            vb = part(2).astype(BF16)
            scores = _dot_nt(qb, kr.astype(BF16)) * dmat_ref[hh]
            s_old = s_scr[hh]
            lhs = jnp.concatenate([scores.astype(BF16), (qr * qdec_ref[hh]).astype(BF16)], axis=1)
            o = _dot(lhs, jnp.concatenate([vb, s_old.astype(BF16)], axis=0))
            kd_t = jnp.transpose(kr * kdec_ref[hh]).astype(BF16)
            s_scr[hh] = cdec_ref[hh] * s_old + _dot(kd_t, vb)
            o_ref[rows, sl] = _head_norm_gate(o, gret_ref[:, sl], part(3)).astype(o_ref.dtype)
        return carry

    _projected_chunks(h_ref, w16_ref, p_even, p_odd, chunk, 0)
    st_ref[0] = s_scr[...]


def _ret_prompt(h, w_t, layer, batch, seq, cos2, sin2, tabs, g_ret):
    hp = 4
    width = hp * HEAD_DIM
    wrow = lambda off: pl.BlockSpec((None, width, D_MODEL), lambda c, b, off=off: (layer, off // width + c, 0),
                                    pipeline_mode=pl.Buffered(1))
    tab = pl.BlockSpec((hp, CHUNK, CHUNK), lambda c, b: (c, 0, 0), pipeline_mode=pl.Buffered(1))
    full = pl.BlockSpec((seq, HEAD_DIM), lambda c, b: (0, 0), pipeline_mode=pl.Buffered(1))
    return pl.pallas_call(
        _ret_prompt_kernel,
        out_shape=(jax.ShapeDtypeStruct((batch * seq, RET_DIM), BF16),
                   jax.ShapeDtypeStruct((batch, RET_HEADS, HEAD_DIM, HEAD_DIM), F32)),
        grid=(RET_HEADS // hp, batch),
        in_specs=[pl.BlockSpec((seq, D_MODEL), lambda c, b: (b, 0)),
                  wrow(OFF_Q), wrow(OFF_K), wrow(OFF_V), wrow(OFF_GR), full, full, tab, tab, tab, tab,
                  pl.BlockSpec((1, width), lambda c, b: (0, c))],
        out_specs=(pl.BlockSpec((seq, width), lambda c, b: (b, c)),
                   pl.BlockSpec((1, hp, HEAD_DIM, HEAD_DIM), lambda c, b: (b, c, 0, 0))),
        scratch_shapes=[pltpu.VMEM((D_MODEL, 4 * width), BF16),
                        pltpu.VMEM((PROJ_ROWS, 4 * width), F32), pltpu.VMEM((PROJ_ROWS, 4 * width), F32),
                        pltpu.VMEM((hp, HEAD_DIM, HEAD_DIM), F32)],
        compiler_params=_params("parallel", "arbitrary"),
        name="ret_prompt",
    )(h, w_t, w_t, w_t, w_t, cos2, sin2, *tabs, g_ret)


def _ret_decode_kernel(q_ref, k_ref, v_ref, g_ref, cos_ref, sin_ref, gam_ref, gret_ref, s_ref, acc_ref, o_ref,
                       sn_ref):
    del acc_ref
    bt = q_ref.shape[0]
    assert 2 * bt == ITEMS
    cos2 = cos_ref[...]
    sin2 = sin_ref[...]
    lane = lax.broadcasted_iota(jnp.int32, (LANES, LANES), 1)
    zero = jnp.zeros((bt, HEAD_DIM), F32)
    one = jnp.ones((bt, HEAD_DIM), F32)
    for hp in range(RET_HEADS // 2):
        qr, kr, v, gam, sls = [], [], [], [], []
        for hh in range(2):
            h = 2 * hp + hh
            sl = slice(h * HEAD_DIM, (h + 1) * HEAD_DIM)
            sls.append(sl)
            gam.append(gam_ref[h][0:1, :])
            qr.append(_rope(q_ref[:, sl], cos2, sin2))
            kr.append(_rope(k_ref[:, sl], cos2, sin2) * (HEAD_DIM ** -0.5))
            v.append(v_ref[:, sl])
        k2 = [_bf16_parts(t, 2) for t in kr]
        q3 = [_bf16_parts(t, 3) for t in qr]
        v2 = [_bf16_parts(t, 2) for t in v]
        both = lambda parts, n: [parts[0][n], parts[1][n]]
        a = jnp.concatenate(both(k2, 0) + both(k2, 0) + both(k2, 1) + both(q3, 0) + both(q3, 1) + both(q3, 2)
                            + [zero] * 4, axis=0)
        cols = jnp.transpose(a)
        r_kv = jnp.concatenate(both(v2, 0) + both(v2, 1) + both(v2, 0) + [zero] * 10, axis=0)
        r_q = jnp.concatenate([zero] * 6 + [one] * 6 + [zero] * 4, axis=0)
        r = jnp.concatenate([r_kv, r_q], axis=1).astype(BF16)
        q_s = [[], []]
        for i in range(ITEMS):
            hh, j = divmod(i, bt)
            h = 2 * hp + hh
            out = _dot(jnp.where((lane & (ITEMS - 1)) == i, cols, 0.0).astype(BF16), r)
            s_old = s_ref[j, h]
            sn_ref[j, h] = gam[hh] * s_old + out[:, :HEAD_DIM]
            q_s[hh].append(jnp.sum(s_old * out[:, HEAD_DIM:], axis=0, keepdims=True))
        for hh in range(2):
            qk = jnp.sum(qr[hh] * kr[hh], axis=-1, keepdims=True)
            o = qk * v[hh] + jnp.concatenate(q_s[hh], axis=0) * gam[hh]
            o_ref[:, sls[hh]] = _head_norm_gate(o, gret_ref[:, sls[hh]], g_ref[:, sls[hh]]).astype(o_ref.dtype)


def _ret_decode(proj, states, acc, layer, cos2, sin2, gam_tab, g_ret, bt):
    nb = proj.shape[0]
    col = lambda off: pl.BlockSpec((bt, RET_DIM), lambda i, off=off: (i, off // RET_DIM))
    slab = pl.BlockSpec((None, bt, RET_HEADS, HEAD_DIM, HEAD_DIM), lambda i: (layer, i, 0, 0, 0))
    return pl.pallas_call(
        _ret_decode_kernel,
        out_shape=(jax.ShapeDtypeStruct((nb, RET_DIM), BF16), jax.ShapeDtypeStruct(states.shape, F32)),
        grid=(nb // bt,),
        in_specs=[col(OFF_Q), col(OFF_K), col(OFF_V), col(OFF_GR),
                  pl.BlockSpec((1, HEAD_DIM), lambda i: (0, 0)), pl.BlockSpec((1, HEAD_DIM), lambda i: (0, 0)),
                  pl.BlockSpec((RET_HEADS, SUBLANES, LANES), lambda i: (0, 0, 0)),
                  pl.BlockSpec((1, RET_DIM), lambda i: (0, 0)), slab, pl.BlockSpec(memory_space=pl.ANY)],
        out_specs=(pl.BlockSpec((bt, RET_DIM), lambda i: (i, 0)), slab),
        input_output_aliases={9: 1},
        compiler_params=_params("parallel"),
        name="ret_decode",
    )(proj, proj, proj, proj, cos2, sin2, gam_tab, g_ret, states, acc)


def _sc_prompt_kernel(h_ref, wbg_ref, wcg_ref, wsh_ref, wgs_ref, w_ref, b_ref, o_ref, st_ref, w16_ref, p_even,
                      p_odd):
    @pl.when(pl.program_id(1) == 0)
    def _():
        _load_weight_cols((wbg_ref, wcg_ref, wsh_ref, wgs_ref), w16_ref)

    width = wbg_ref.shape[0]

    def chunk(p, r0, n, tail):
        rows = pl.ds(pl.multiple_of(n * CHUNK, CHUNK), CHUNK)
        part = lambda i: p[r0:r0 + CHUNK, i * width:(i + 1) * width]
        u = part(1) * part(2)
        conv = _causal_conv_rows(tail, u, w_ref, b_ref)
        o_ref[rows, :] = (part(0) * conv * _silu(part(3))).astype(o_ref.dtype)
        return u[CHUNK - SUBLANES:, :]

    tail = _projected_chunks(h_ref, w16_ref, p_even, p_odd, chunk, jnp.zeros((SUBLANES, width), F32))
    st_ref[0] = tail[SUBLANES - (SC_WIDTH - 1):, :]


def _sc_prompt(h, w_t, layer, batch, seq, w, b):
    width = 512
    proj_rows = min(2 * PROJ_ROWS, seq)
    wrow = lambda off: pl.BlockSpec((None, width, D_MODEL), lambda c, bi, off=off: (layer, off // width + c, 0),
                                    pipeline_mode=pl.Buffered(1))
    return pl.pallas_call(
        _sc_prompt_kernel,
        out_shape=(jax.ShapeDtypeStruct((batch * seq, SC_DIM), BF16),
                   jax.ShapeDtypeStruct((batch, SC_WIDTH - 1, SC_DIM), F32)),
        grid=(SC_DIM // width, batch),
        in_specs=[pl.BlockSpec((seq, D_MODEL), lambda c, bi: (bi, 0)),
                  wrow(OFF_BG), wrow(OFF_CG), wrow(OFF_SH), wrow(OFF_GS),
                  pl.BlockSpec((SC_WIDTH, width), lambda c, bi: (0, c)),
                  pl.BlockSpec((1, width), lambda c, bi: (0, c))],
        out_specs=(pl.BlockSpec((seq, width), lambda c, bi: (bi, c)),
                   pl.BlockSpec((1, SC_WIDTH - 1, width), lambda c, bi: (bi, 0, c))),
        scratch_shapes=[pltpu.VMEM((D_MODEL, 4 * width), BF16),
                        pltpu.VMEM((proj_rows, 4 * width), F32), pltpu.VMEM((proj_rows, 4 * width), F32)],
        compiler_params=_params("parallel", "arbitrary"),
        name="sc_prompt",
    )(h, w_t, w_t, w_t, w_t, w, b)


def _sc_decode_kernel(bg_ref, cg_ref, sh_ref, gs_ref, st_ref, w_ref, b_ref, o_ref, stn_ref):
    u = cg_ref[...] * sh_ref[...]
    r0 = st_ref[:, 0:SC_DIM]
    r1 = st_ref[:, SC_DIM:]
    conv = b_ref[...] + r0 * w_ref[0:1, :] + r1 * w_ref[1:2, :] + u * w_ref[2:3, :]
    o_ref[...] = (bg_ref[...] * conv * _silu(gs_ref[...])).astype(o_ref.dtype)
    stn_ref[:, 0:SC_DIM] = r1
    stn_ref[:, SC_DIM:] = u


def _sc_decode(proj, state2d, w, b):
    nb = proj.shape[0]
    col = lambda off: pl.BlockSpec((nb, SC_DIM), lambda i, off=off: (0, off // SC_DIM))
    whole = lambda shape: pl.BlockSpec(shape, lambda i: (0,) * len(shape))
    return pl.pallas_call(
        _sc_decode_kernel,
        out_shape=(jax.ShapeDtypeStruct((nb, SC_DIM), BF16), jax.ShapeDtypeStruct(state2d.shape, F32)),
        grid=(1,),
        in_specs=[col(OFF_BG), col(OFF_CG), col(OFF_SH), col(OFF_GS), whole(state2d.shape),
                  whole((SC_WIDTH, SC_DIM)), whole((1, SC_DIM))],
        out_specs=(whole((nb, SC_DIM)), whole(state2d.shape)),
        compiler_params=_params("arbitrary"),
        name="sc_decode",
    )(proj, proj, proj, proj, state2d, w, b)


def _lane_cumsum(a, lane):
    s = 1
    while s < CHUNK:
        a = a + jnp.where(lane >= s, pltpu.roll(a, s, 1), 0.0)
        s *= 2
    return a


def _ssd_prompt_kernel(h_ref, wz_ref, wx_ref, wb_ref, wc_ref, wdt_ref, dtb_ref, alog_ref, dsk_ref,
                       cwx_ref, cwb_ref, cwc_ref, cbx_ref, cbb_ref, cbc_ref,
                       y_ref, stx_ref, stb_ref, stc_ref, sst_ref, w16_ref, p_even, p_odd, s_scr):
    @pl.when(pl.program_id(1) == 0)
    def _():
        _load_weight_cols((wz_ref, wx_ref, wb_ref, wc_ref, wdt_ref), w16_ref)

    gw = wz_ref.shape[0]
    n_st = wb_ref.shape[0]
    off_x, off_b, off_c, off_dt = gw, 2 * gw, 2 * gw + n_st, 2 * gw + 2 * n_st
    row = lax.broadcasted_iota(jnp.int32, (CHUNK, CHUNK), 0)
    lane = lax.broadcasted_iota(jnp.int32, (CHUNK, CHUNK), 1)
    lane8 = lax.broadcasted_iota(jnp.int32, (HEADS_PER_GROUP, CHUNK), 1)
    causal = row >= lane
    low_lanes = lane < SSM_HEAD_DIM
    low_rows = row < SSM_HEAD_DIM
    a_neg = -jnp.exp(alog_ref[0])
    dt_bias = dtb_ref[0]
    pad_heads = jnp.zeros((CHUNK - HEADS_PER_GROUP, CHUNK), F32)
    s_scr[...] = jnp.zeros_like(s_scr)

    def chunk(p, r0, n, tails):
        tx, tb, tc = tails
        rows = pl.ds(pl.multiple_of(n * CHUNK, CHUNK), CHUNK)
        loc = slice(r0, r0 + CHUNK)
        x_raw = p[loc, off_x:off_x + gw]
        b_raw = p[loc, off_b:off_b + n_st]
        c_raw = p[loc, off_c:off_c + n_st]
        xc = _silu(_causal_conv_rows(tx, x_raw, cwx_ref, cbx_ref))
        bb = _silu(_causal_conv_rows(tb, b_raw, cwb_ref, cbb_ref)).astype(BF16)
        cb16 = _silu(_causal_conv_rows(tc, c_raw, cwc_ref, cbc_ref)).astype(BF16)
        dt_t = _softplus(jnp.transpose(p[loc, off_dt:off_dt + LANES])[0:HEADS_PER_GROUP, :] + dt_bias)
        acum_t = _lane_cumsum(dt_t * a_neg, lane8)
        acum = jnp.transpose(jnp.concatenate([acum_t, pad_heads], axis=0))
        cb = _dot_nt(cb16, bb)
        y_intra, e_sel, dec_sel, xw, s_old = [], [], [], [], []
        for pp in range(PAIRS_PER_GROUP):
            x_pair = xc[:, pp * PAIR:(pp + 1) * PAIR]
            m, e_col, w_row, dec = [], [], [], []
            for hh in range(2):
                c = 2 * pp + hh
                a_col = jnp.broadcast_to(acum[:, c:c + 1], (CHUNK, CHUNK))
                a_row = acum_t[c:c + 1, :]
                dt_row = dt_t[c:c + 1, :]
                lmat = jnp.exp(jnp.where(causal, a_col - a_row, -jnp.inf))
                m.append((cb * lmat * dt_row).astype(BF16))
                e_col.append(jnp.exp(a_col))
                last = a_col[CHUNK - 1:CHUNK, :]
                w_row.append(jnp.exp(last - a_row) * dt_row)
                dec.append(jnp.exp(last))
            x_stack = jnp.concatenate([jnp.where(low_lanes, x_pair, 0.0), jnp.where(low_lanes, 0.0, x_pair)], axis=0)
            y_intra.append(_dot(jnp.concatenate(m, axis=1), x_stack.astype(BF16)))
            e_sel.append(jnp.where(low_lanes, e_col[0], e_col[1]))
            dec_sel.append(jnp.where(low_rows, dec[0], dec[1]))
            xw.append((jnp.transpose(x_pair) * jnp.where(low_rows, w_row[0], w_row[1])).astype(BF16))
            s_old.append(s_scr[pp])
        s_upd = _dot(jnp.concatenate(xw, axis=0), bb)
        for q in range(PAIRS_PER_GROUP // 2):
            s_two = jnp.concatenate([s_old[2 * q], s_old[2 * q + 1]], axis=0).astype(BF16)
            y_two = _dot_nt(cb16, s_two)
            for r in range(2):
                pp = 2 * q + r
                sl = slice(pp * PAIR, (pp + 1) * PAIR)
                y = y_intra[pp] + y_two[:, r * PAIR:(r + 1) * PAIR] * e_sel[pp] + dsk_ref[0][:, sl] * xc[:, sl]
                y_ref[rows, sl] = y * _silu(p[loc, sl])
                s_scr[pp] = dec_sel[pp] * s_old[pp] + s_upd[pp * PAIR:(pp + 1) * PAIR, :]
        cut = CHUNK - SUBLANES
        return x_raw[cut:, :], b_raw[cut:, :], c_raw[cut:, :]

    zeros = lambda width: jnp.zeros((SUBLANES, width), F32)
    tx, tb, tc = _projected_chunks(h_ref, w16_ref, p_even, p_odd, chunk, (zeros(gw), zeros(n_st), zeros(n_st)))
    keep = SUBLANES - (SSM_CONV - 1)
    stx_ref[0] = tx[keep:, :]
    stb_ref[0] = tb[keep:, :]
    stc_ref[0] = tc[keep:, :]
    for pp in range(PAIRS_PER_GROUP):
        for hh in range(2):
            sst_ref[0, 2 * pp + hh] = s_scr[pp, hh * SSM_HEAD_DIM:(hh + 1) * SSM_HEAD_DIM, :]


def _ssd_prompt(h, w_t, layer, batch, seq, dtb, alog, dsk, conv_w, conv_b):
    gw = SSM_DIM // SSM_GROUPS
    n = SSM_STATE
    wrow = lambda off, width: pl.BlockSpec((None, width, D_MODEL),
                                           lambda g, b, off=off, width=width: (layer, off // width + g, 0),
                                           pipeline_mode=pl.Buffered(1))
    grp = lambda rows, width: pl.BlockSpec((1, rows, width), lambda g, b: (g, 0, 0))
    cw = lambda off, width: pl.BlockSpec((SSM_CONV, width), lambda g, b, off=off, width=width: (0, off // width + g))
    cbias = lambda off, width: pl.BlockSpec((1, width), lambda g, b, off=off, width=width: (0, off // width + g))
    st = lambda width: pl.BlockSpec((1, SSM_CONV - 1, width), lambda g, b: (b, 0, g))
    n_proj = 2 * gw + 2 * n + LANES
    return pl.pallas_call(
        _ssd_prompt_kernel,
        out_shape=(jax.ShapeDtypeStruct((batch * seq, SSM_DIM), F32),
                   jax.ShapeDtypeStruct((batch, SSM_CONV - 1, SSM_DIM), F32),
                   jax.ShapeDtypeStruct((batch, SSM_CONV - 1, SSM_GROUPS * n), F32),
                   jax.ShapeDtypeStruct((batch, SSM_CONV - 1, SSM_GROUPS * n), F32),
                   jax.ShapeDtypeStruct((batch, SSM_HEADS, SSM_HEAD_DIM, n), F32)),
        grid=(SSM_GROUPS, batch),
        in_specs=[pl.BlockSpec((seq, D_MODEL), lambda g, b: (b, 0)),
                  wrow(OFF_Z, gw), wrow(OFF_X, gw), wrow(OFF_B, n), wrow(OFF_C, n), wrow(OFF_DT, HEADS_PER_GROUP),
                  grp(HEADS_PER_GROUP, CHUNK), grp(HEADS_PER_GROUP, CHUNK), grp(1, gw),
                  cw(0, gw), cw(SSM_DIM, n), cw(SSM_DIM + SSM_GROUPS * n, n),
                  cbias(0, gw), cbias(SSM_DIM, n), cbias(SSM_DIM + SSM_GROUPS * n, n)],
        out_specs=(pl.BlockSpec((seq, gw), lambda g, b: (b, g)), st(gw), st(n), st(n),
                   pl.BlockSpec((1, HEADS_PER_GROUP, SSM_HEAD_DIM, n), lambda g, b: (b, g, 0, 0))),
        scratch_shapes=[pltpu.VMEM((D_MODEL, n_proj), BF16),
                        pltpu.VMEM((PROJ_ROWS, n_proj), F32), pltpu.VMEM((PROJ_ROWS, n_proj), F32),
                        pltpu.VMEM((PAIRS_PER_GROUP, PAIR, n), F32)],
        compiler_params=_params("parallel", "arbitrary"),
        name="ssd_prompt",
    )(h, w_t, w_t, w_t, w_t, w_t, dtb, alog, dsk, conv_w, conv_w, conv_w, conv_b, conv_b, conv_b)


def _ssd_decode_kernel(z_ref, x_ref, bc_ref, dtr_ref, cst_ref, s_ref, cw_ref, cb_ref, dtb_ref, alog_ref, dsk_ref,
                       acc_ref, y_ref, cstn_ref, sn_ref):
    del acc_ref
    bt = z_ref.shape[0]
    xr = SSM_DIM // LANES
    gr = SSM_GROUPS * SSM_STATE // LANES
    u = jnp.concatenate([x_ref[...], bc_ref[...]], axis=1)
    conv = cb_ref[...]
    for j in range(SSM_CONV - 1):
        conv = conv + cst_ref[:, j] * cw_ref[j]
        cstn_ref[:, j] = u if j == SSM_CONV - 2 else cst_ref[:, j + 1]
    act = _silu(conv + u * cw_ref[SSM_CONV - 1])
    xs = act[:, 0:xr]
    bm = act[:, xr:xr + gr]
    cm = act[:, xr + gr:]
    dt = _softplus(dtr_ref[...] + dtb_ref[...])
    ea = jnp.exp(dt * -jnp.exp(alog_ref[...]))
    xdt = xs * dt
    assert xr == ITEMS == N_PAIRS
    lane = lax.broadcasted_iota(jnp.int32, (LANES, LANES), 1)
    row16 = lax.broadcasted_iota(jnp.int32, (xr, LANES), 0)
    zeros = lambda n: jnp.zeros((n, LANES), F32)
    r_ea = jnp.concatenate([zeros(3 * ITEMS), jnp.ones((3 * ITEMS, LANES), F32), zeros(2 * ITEMS)], axis=0)
    for j in range(bt):
        cbt = zeros(xr)
        b_rows = zeros(xr)
        for g in range(SSM_GROUPS):
            in_group = row16 // PAIRS_PER_GROUP == g
            cb_g = jnp.sum(cm[j, g:g + 1, :] * bm[j, g:g + 1, :], axis=-1, keepdims=True)
            cbt = jnp.where(in_group, cb_g, cbt)
            b_rows = jnp.where(in_group, bm[j, g:g + 1, :], b_rows)
        xd2 = _bf16_parts(xdt[j], 2)
        ea3 = _bf16_parts(ea[j], 3)
        b2 = _bf16_parts(b_rows, 2)
        a = jnp.concatenate([xd2[0], xd2[0], xd2[1]] + ea3 + [zeros(2 * ITEMS)], axis=0)
        cols = jnp.transpose(a)
        r_xb = jnp.concatenate([b2[0], b2[1], b2[0], zeros(5 * ITEMS)], axis=0)
        r = jnp.concatenate([r_xb, r_ea], axis=1).astype(BF16)
        y_t = jnp.zeros((LANES, LANES), F32)
        for pp in range(N_PAIRS):
            g = pp // PAIRS_PER_GROUP
            out = _dot(jnp.where((lane & (ITEMS - 1)) == pp, cols, 0.0).astype(BF16), r)
            s_old = s_ref[j, 2 * pp:2 * pp + 2].reshape(PAIR, SSM_STATE)
            y_col = jnp.sum(s_old * cm[j, g:g + 1, :], axis=-1, keepdims=True)
            y_t = jnp.where(lane == pp, y_col, y_t)
            s_new = out[:, SSM_STATE:] * s_old + out[:, :SSM_STATE]
            sn_ref[j, 2 * pp:2 * pp + 2] = s_new.reshape(2, SSM_HEAD_DIM, SSM_STATE)
        y_inter = jnp.transpose(y_t)[0:xr, :]
        y = cbt * xdt[j] + y_inter * ea[j] + dsk_ref[0] * xs[j]
        y_ref[j] = y * _silu(z_ref[j])


def _ssd_decode(proj3, dt_raw3, cstates, states, acc, layer, conv_w3, conv_b3, dtb3, alog3, dsk3, bt):
    nb = proj3.shape[0]
    xr = SSM_DIM // LANES
    cr = SSM_CONV_DIM // LANES
    whole = lambda shape: pl.BlockSpec(shape, lambda i: (0,) * len(shape))
    cst = pl.BlockSpec((bt, SSM_CONV - 1, cr, LANES), lambda i: (i, 0, 0, 0))
    st_in = pl.BlockSpec((None, bt, SSM_HEADS, SSM_HEAD_DIM, SSM_STATE), lambda i: (layer, i, 0, 0, 0))
    cst_in = pl.BlockSpec((None, bt, SSM_CONV - 1, cr, LANES), lambda i: (layer, i, 0, 0, 0))
    return pl.pallas_call(
        _ssd_decode_kernel,
        out_shape=(jax.ShapeDtypeStruct((nb, xr, LANES), F32), jax.ShapeDtypeStruct(cstates.shape[1:], F32),
                   jax.ShapeDtypeStruct(states.shape, F32)),
        grid=(nb // bt,),
        in_specs=[pl.BlockSpec((bt, xr, LANES), lambda i: (i, OFF_Z // SSM_DIM, 0)),
                  pl.BlockSpec((bt, xr, LANES), lambda i: (i, OFF_X // SSM_DIM, 0)),
                  pl.BlockSpec((bt, SUBLANES, LANES), lambda i: (i, OFF_B // (SUBLANES * LANES), 0)),
                  pl.BlockSpec((bt, xr, LANES), lambda i: (i, 0, 0)),
                  cst_in, st_in, whole((SSM_CONV, cr, LANES)), whole((1, cr, LANES)),
                  whole((1, xr, LANES)), whole((1, xr, LANES)), whole((1, xr, LANES)),
                  pl.BlockSpec(memory_space=pl.ANY)],
        out_specs=(pl.BlockSpec((bt, xr, LANES), lambda i: (i, 0, 0)), cst, st_in),
        input_output_aliases={11: 2},
        compiler_params=_params("parallel"),
        name="ssd_decode",
    )(proj3, proj3, proj3, dt_raw3, cstates, states, conv_w3, conv_b3, dtb3, alog3, dsk3, acc)


def _rope_tables(pos):
    half = HEAD_DIM // 2
    inv = ROPE_BASE ** (-jnp.arange(half, dtype=F32) / half)
    ang = pos.astype(F32)[:, None] * inv[None, :]
    cos, sin = jnp.cos(ang), jnp.sin(ang)
    return jnp.concatenate([cos, cos], axis=-1), jnp.concatenate([-sin, sin], axis=-1)


def _retention_tables():
    lg = jnp.log(1.0 - 2.0 ** (-5.0 - jnp.arange(RET_HEADS, dtype=F32)))
    i = jnp.arange(CHUNK, dtype=F32)
    diff = i[:, None] - i[None, :]
    dmat = jnp.exp(jnp.where((diff >= 0)[None], diff[None] * lg[:, None, None], -jnp.inf))
    full = lambda t: jnp.broadcast_to(t, (RET_HEADS, CHUNK, CHUNK))
    q_dec = full(jnp.exp((i + 1.0)[None, :, None] * lg[:, None, None]))
    k_dec = full(jnp.exp((CHUNK - 1.0 - i)[None, :, None] * lg[:, None, None]))
    c_dec = full(jnp.exp(CHUNK * lg)[:, None, None])
    gam = jnp.broadcast_to(jnp.exp(lg)[:, None, None], (RET_HEADS, SUBLANES, LANES))
    return (dmat, q_dec, k_dec, c_dec), gam


def _per_group(v):
    v = v.astype(F32).reshape(SSM_GROUPS, HEADS_PER_GROUP, 1)
    return jnp.broadcast_to(v, (SSM_GROUPS, HEADS_PER_GROUP, CHUNK))


def _per_lane(v):
    return jnp.repeat(v.astype(F32), SSM_HEAD_DIM)[None, :]


def kernel(x_prompt, x_sample, state_ret, state_sconv, state_ssm_conv, state_ssm, w_in, w_out, norm_pre, norm_post,
           ret_norm, sc_conv_w, sc_conv_b, ssm_conv_w, ssm_conv_b, ssm_dt_bias, ssm_a_log, ssm_d, ssm_norm):
    batch, seq, _ = x_prompt.shape
    nb = x_sample.shape[0]
    depth = w_in.shape[0]
    mp = batch * seq
    xr = SSM_DIM // LANES
    cr = SSM_CONV_DIM // LANES

    cos_p, sin_p = _rope_tables(jnp.arange(seq))
    cos_s, sin_s = _rope_tables(PAST_LEN + jnp.arange(1))
    ret_tabs, gam_tab = _retention_tables()

    w_out16 = w_out.astype(BF16)
    w_t = jnp.swapaxes(w_in, 1, 2)
    w_dt_lane = jnp.repeat(w_t[:, OFF_DT:, :], SSM_HEAD_DIM, axis=1).astype(BF16)

    tm_o = min(2 * OUT_SUB_ROWS, mp)
    xp = x_prompt.reshape(mp, D_MODEL)
    xs = x_sample.reshape(nb, D_MODEL)
    hp = _prenorm(xp, norm_pre[0][None, :], tm_o)
    hs = _prenorm(xs, norm_pre[0][None, :], nb)

    outs = [[] for _ in range(6)]
    ret_acc = jnp.zeros(state_ret.shape, F32)
    ssm_acc = jnp.zeros(state_ssm.shape, F32)
    for l in range(depth):
        g_next = norm_pre[(l + 1) % depth][None, :]
        dtb_g, alog_g = _per_group(ssm_dt_bias[l]), _per_group(ssm_a_log[l])
        dsk_lane = _per_lane(ssm_d[l])
        dsk_g = dsk_lane.reshape(SSM_GROUPS, 1, SSM_DIM // SSM_GROUPS)
        conv_b = ssm_conv_b[l][None, :]

        o_ret, r_new = _ret_prompt(hp, w_t, l, batch, seq, cos_p, sin_p, ret_tabs, ret_norm[l][None, :])
        o_sc, c_new = _sc_prompt(hp, w_t, l, batch, seq, sc_conv_w[l], sc_conv_b[l][None, :])
        ssm_pre, cx, cb_, cc, s_new = _ssd_prompt(hp, w_t, l, batch, seq, dtb_g, alog_g, dsk_g, ssm_conv_w[l], conv_b)
        xp, hp = _outproj(o_ret, o_sc, ssm_pre, xp, w_out16, l, ssm_norm[l][None, :], norm_post[l][None, :],
                          g_next, tm_o)
        outs[0].append(r_new)
        outs[1].append(c_new)
        outs[2].append(jnp.concatenate([cx, cb_, cc], axis=-1))
        outs[3].append(s_new)

        proj_s = _matmul_nt(hs, w_t, l, N_MAIN, 1024, "inproj_decode")
        dt_s = _matmul_nt(hs, w_dt_lane, l, SSM_DIM, SSM_DIM, "dtproj_decode")
        o_ret, ret_acc = _ret_decode(proj_s, state_ret, ret_acc, l, cos_s, sin_s, gam_tab, ret_norm[l][None, :],
                                     SUBLANES)
        o_sc, c_new = _sc_decode(proj_s, state_sconv[l].reshape(nb, (SC_WIDTH - 1) * SC_DIM), sc_conv_w[l],
                                 sc_conv_b[l][None, :])
        y_s, cs_new, ssm_acc = _ssd_decode(
            proj_s.reshape(nb, N_MAIN // LANES, LANES), dt_s.reshape(nb, xr, LANES),
            state_ssm_conv.reshape(depth, nb, SSM_CONV - 1, cr, LANES), state_ssm, ssm_acc, l,
            ssm_conv_w[l].reshape(SSM_CONV, cr, LANES), conv_b.reshape(1, cr, LANES),
            _per_lane(ssm_dt_bias[l]).reshape(1, xr, LANES), _per_lane(ssm_a_log[l]).reshape(1, xr, LANES),
            dsk_lane.reshape(1, xr, LANES), 4)
        xs, hs = _outproj(o_ret, o_sc, y_s.reshape(nb, SSM_DIM), xs, w_out16, l, ssm_norm[l][None, :],
                          norm_post[l][None, :], g_next, nb)
        outs[4].append(c_new.reshape(nb, SC_WIDTH - 1, SC_DIM))
        outs[5].append(cs_new.reshape(nb, SSM_CONV - 1, SSM_CONV_DIM))

    stacked = [jnp.stack(o) for o in outs]
    return (xp.reshape(batch, seq, D_MODEL), xs.reshape(nb, 1, D_MODEL), *stacked[:4],
            ret_acc, stacked[4], stacked[5], ssm_acc)
```

```python
import functools

import jax
import jax.numpy as jnp
import numpy as np
from jax import lax
from jax.experimental import pallas as pl
from jax.experimental.pallas import tpu as pltpu

F32 = jnp.float32
BF16 = jnp.bfloat16

D_MODEL = 2048
D_MIX = 2 * D_MODEL
RET_HEADS = 8
RET_DIM = 1024
HEAD_DIM = 128
SC_DIM = 1024
SC_WIDTH = 3
SSM_DIM = 2048
SSM_HEAD_DIM = 64
SSM_HEADS = 32
SSM_GROUPS = 4
SSM_STATE = 128
SSM_CONV = 4
SSM_CONV_DIM = SSM_DIM + 2 * SSM_GROUPS * SSM_STATE
CHUNK = 128
PROJ_ROWS = 2 * CHUNK
OUT_SUB_ROWS = 256
ROPE_BASE = 10000.0
EPS = 1e-6
PAST_LEN = 16384

OFF_Q, OFF_K, OFF_V, OFF_GR = 0, 1024, 2048, 3072
OFF_BG, OFF_CG, OFF_SH, OFF_GS = 4096, 5120, 6144, 7168
OFF_Z, OFF_X, OFF_B, OFF_C, OFF_DT = 8192, 10240, 12288, 12800, 13312
N_MAIN = OFF_DT

LANES = 128
SUBLANES = 8
V7X_VMEM_LIMIT = 60 * 1024 * 1024
PAIR = 2 * SSM_HEAD_DIM
HEADS_PER_GROUP = SSM_HEADS // SSM_GROUPS
PAIRS_PER_GROUP = HEADS_PER_GROUP // 2
N_PAIRS = SSM_HEADS // 2
ITEMS = 16


def _params(*sem):
    return pltpu.CompilerParams(dimension_semantics=sem, vmem_limit_bytes=V7X_VMEM_LIMIT)


def _silu(x):
    return x * jax.nn.sigmoid(x)


def _softplus(x):
    return jnp.maximum(x, 0.0) + jnp.log1p(jnp.exp(-jnp.abs(x)))


def _dot(a, b):
    return jnp.dot(a, b, preferred_element_type=F32)


def _bf16_parts(x, n):
    parts = []
    for _ in range(n):
        p = x.astype(BF16).astype(F32)
        parts.append(p)
        x = x - p
    return parts


def _dot_nt(a, b):
    return lax.dot_general(a, b, (((1,), (1,)), ((), ())), preferred_element_type=F32)


def _prenorm_kernel(x_ref, g_ref, o_ref):
    x = x_ref[...]
    ms = jnp.mean(x * x, axis=-1, keepdims=True)
    o_ref[...] = (x * lax.rsqrt(ms + EPS) * g_ref[...]).astype(o_ref.dtype)


def _prenorm(x, g, tm):
    m, d = x.shape
    return pl.pallas_call(
        _prenorm_kernel,
        out_shape=jax.ShapeDtypeStruct((m, d), BF16),
        grid=(m // tm,),
        in_specs=[pl.BlockSpec((tm, d), lambda i: (i, 0)), pl.BlockSpec((1, d), lambda i: (0, 0))],
        out_specs=pl.BlockSpec((tm, d), lambda i: (i, 0)),
        compiler_params=_params("parallel"),
        name="prenorm",
    )(x, g)


def _mm_nt_kernel(x_ref, w_ref, o_ref):
    o_ref[...] = _dot_nt(x_ref[...], w_ref[...].astype(BF16))


def _matmul_nt(x, w_t, layer, n_cols, tn, name):
    m, k = x.shape
    return pl.pallas_call(
        _mm_nt_kernel,
        out_shape=jax.ShapeDtypeStruct((m, n_cols), F32),
        grid=(n_cols // tn,),
        in_specs=[pl.BlockSpec((m, k), lambda j: (0, 0)),
                  pl.BlockSpec((None, tn, k), lambda j: (layer, j, 0))],
        out_specs=pl.BlockSpec((m, tn), lambda j: (0, j)),
        compiler_params=_params("parallel"),
        name=name,
    )(x, w_t)


def _outproj_kernel(oret_ref, osc_ref, ssm_ref, x_ref, w_ref, gssm_ref, gpost_ref, gnext_ref, y_ref, hn_ref):
    tm = x_ref.shape[0]
    sub = min(tm, OUT_SUB_ROWS)
    for r0 in range(0, tm, sub):
        rows = slice(r0, r0 + sub)
        ypre = ssm_ref[rows, :]
        ms = jnp.mean(ypre * ypre, axis=-1, keepdims=True)
        ossm = (ypre * lax.rsqrt(ms + EPS) * gssm_ref[...]).astype(BF16)
        acc = _dot(oret_ref[rows, :], w_ref[0:RET_DIM, :])
        acc = acc + _dot(osc_ref[rows, :], w_ref[RET_DIM:RET_DIM + SC_DIM, :])
        acc = acc + _dot(ossm, w_ref[RET_DIM + SC_DIM:, :])
        ms2 = jnp.mean(acc * acc, axis=-1, keepdims=True)
        y = x_ref[rows, :] + acc * lax.rsqrt(ms2 + EPS) * gpost_ref[...]
        y_ref[rows, :] = y
        ms3 = jnp.mean(y * y, axis=-1, keepdims=True)
        hn_ref[rows, :] = (y * lax.rsqrt(ms3 + EPS) * gnext_ref[...]).astype(hn_ref.dtype)


def _outproj(oret, osc, ssm_pre, x, w_out, layer, g_ssm, g_post, g_next, tm):
    m = x.shape[0]
    row = lambda width: pl.BlockSpec((tm, width), lambda i: (i, 0))
    vec = lambda width: pl.BlockSpec((1, width), lambda i: (0, 0))
    return pl.pallas_call(
        _outproj_kernel,
        out_shape=(jax.ShapeDtypeStruct((m, D_MODEL), F32), jax.ShapeDtypeStruct((m, D_MODEL), BF16)),
        grid=(m // tm,),
        in_specs=[row(RET_DIM), row(SC_DIM), row(SSM_DIM), row(D_MODEL),
                  pl.BlockSpec((None, D_MIX, D_MODEL), lambda i: (layer, 0, 0), pipeline_mode=pl.Buffered(1)),
                  vec(SSM_DIM), vec(D_MODEL), vec(D_MODEL)],
        out_specs=(row(D_MODEL), row(D_MODEL)),
        compiler_params=_params("parallel"),
        name="outproj",
    )(oret, osc, ssm_pre, x, w_out, g_ssm, g_post, g_next)


def _rope(t, cos2, sin2):
    return t * cos2 + pltpu.roll(t, HEAD_DIM // 2, 1) * sin2


def _head_norm_gate(o, g_row, gate):
    mu = jnp.mean(o, axis=-1, keepdims=True)
    oc = o - mu
    var = jnp.mean(oc * oc, axis=-1, keepdims=True)
    return oc * lax.rsqrt(var + EPS) * g_row * _silu(gate)


def _shift_rows(tail, u, s):
    ext = jnp.concatenate([tail, u], axis=0)
    return ext[SUBLANES - s:SUBLANES - s + u.shape[0]]


def _causal_conv_rows(tail, u, w_ref, b_ref):
    width = w_ref.shape[0]
    out = b_ref[...]
    for j in range(width - 1):
        out = out + _shift_rows(tail, u, width - 1 - j) * w_ref[j:j + 1, :]
    return out + u * w_ref[width - 1:width, :]


def _load_weight_cols(w_refs, w16_ref):
    off = 0
    for w_ref in w_refs:
        n = w_ref.shape[0]
        for r in range(0, n, LANES):
            blk = w_ref[r:min(r + LANES, n), :]
            if blk.shape[0] < LANES:
                blk = jnp.concatenate([blk, jnp.zeros((LANES - blk.shape[0], blk.shape[1]), F32)], axis=0)
            w16_ref[:, off:off + LANES] = jnp.transpose(blk).astype(BF16)
            off += LANES


def _projected_chunks(h_ref, w16_ref, p_even, p_odd, chunk_fn, carry):
    proj_rows = p_even.shape[0]
    assert h_ref.shape[0] % proj_rows == 0 and proj_rows % CHUNK == 0
    steps = h_ref.shape[0] // proj_rows
    per_step = proj_rows // CHUNK

    def project(step, p_ref):
        rows = pl.ds(pl.multiple_of(step * proj_rows, proj_rows), proj_rows)
        p_ref[...] = _dot(h_ref[rows, :], w16_ref[...])

    def mix(step, p_ref, carry):
        for c in range(per_step):
            carry = chunk_fn(p_ref, c * CHUNK, step * per_step + c, carry)
        return carry

    def body(i, carry):
        project(2 * i + 1, p_odd)
        carry = mix(2 * i, p_even, carry)
        project(2 * i + 2, p_even)
        return mix(2 * i + 1, p_odd, carry)

    project(0, p_even)
    full = (steps - 1) // 2
    carry = lax.fori_loop(0, full, body, carry)
    if steps - 2 * full == 2:
        project(2 * full + 1, p_odd)
        carry = mix(2 * full, p_even, carry)
        return mix(2 * full + 1, p_odd, carry)
    return mix(2 * full, p_even, carry)


def _ret_prompt_kernel(h_ref, wq_ref, wk_ref, wv_ref, wg_ref, cos_ref, sin_ref, dmat_ref, qdec_ref, kdec_ref,
                       cdec_ref, gret_ref, o_ref, st_ref, w16_ref, p_even, p_odd, s_scr):
    @pl.when(pl.program_id(1) == 0)
    def _():
        _load_weight_cols((wq_ref, wk_ref, wv_ref, wg_ref), w16_ref)

    width = wq_ref.shape[0]
    heads = width // HEAD_DIM
    s_scr[...] = jnp.zeros_like(s_scr)

    def chunk(p, r0, n, carry):
        rows = pl.ds(pl.multiple_of(n * CHUNK, CHUNK), CHUNK)
        loc = slice(r0, r0 + CHUNK)
        cos2 = cos_ref[rows, :]
        sin2 = sin_ref[rows, :]
        for hh in range(heads):
            sl = slice(hh * HEAD_DIM, (hh + 1) * HEAD_DIM)
            part = lambda i: p[loc, i * width + hh * HEAD_DIM:i * width + (hh + 1) * HEAD_DIM]
            qr = _rope(part(0), cos2, sin2)
            kr = _rope(part(1), cos2, sin2) * (HEAD_DIM ** -0.5)
            qb = qr.astype(BF16)
            vb = part(2).astype(BF16)
            scores = _dot_nt(qb, kr.astype(BF16)) * dmat_ref[hh]
            s_old = s_scr[hh]
            lhs = jnp.concatenate([scores.astype(BF16), (qr * qdec_ref[hh]).astype(BF16)], axis=1)
            o = _dot(lhs, jnp.concatenate([vb, s_old.astype(BF16)], axis=0))
            kd_t = jnp.transpose(kr * kdec_ref[hh]).astype(BF16)
            s_scr[hh] = cdec_ref[hh] * s_old + _dot(kd_t, vb)
            o_ref[rows, sl] = _head_norm_gate(o, gret_ref[:, sl], part(3)).astype(o_ref.dtype)
        return carry

    _projected_chunks(h_ref, w16_ref, p_even, p_odd, chunk, 0)
    st_ref[0] = s_scr[...]


def _ret_prompt(h, w_t, layer, batch, seq, cos2, sin2, tabs, g_ret):
    hp = 4
    width = hp * HEAD_DIM
    wrow = lambda off: pl.BlockSpec((None, width, D_MODEL), lambda c, b, off=off: (layer, off // width + c, 0),
                                    pipeline_mode=pl.Buffered(1))
    tab = pl.BlockSpec((hp, CHUNK, CHUNK), lambda c, b: (c, 0, 0), pipeline_mode=pl.Buffered(1))
    full = pl.BlockSpec((seq, HEAD_DIM), lambda c, b: (0, 0), pipeline_mode=pl.Buffered(1))
    return pl.pallas_call(
        _ret_prompt_kernel,
        out_shape=(jax.ShapeDtypeStruct((batch * seq, RET_DIM), BF16),
                   jax.ShapeDtypeStruct((batch, RET_HEADS, HEAD_DIM, HEAD_DIM), F32)),
        grid=(RET_HEADS // hp, batch),
        in_specs=[pl.BlockSpec((seq, D_MODEL), lambda c, b: (b, 0)),
                  wrow(OFF_Q), wrow(OFF_K), wrow(OFF_V), wrow(OFF_GR), full, full, tab, tab, tab, tab,
                  pl.BlockSpec((1, width), lambda c, b: (0, c))],
        out_specs=(pl.BlockSpec((seq, width), lambda c, b: (b, c)),
                   pl.BlockSpec((1, hp, HEAD_DIM, HEAD_DIM), lambda c, b: (b, c, 0, 0))),
        scratch_shapes=[pltpu.VMEM((D_MODEL, 4 * width), BF16),
                        pltpu.VMEM((PROJ_ROWS, 4 * width), F32), pltpu.VMEM((PROJ_ROWS, 4 * width), F32),
                        pltpu.VMEM((hp, HEAD_DIM, HEAD_DIM), F32)],
        compiler_params=_params("parallel", "arbitrary"),
        name="ret_prompt",
    )(h, w_t, w_t, w_t, w_t, cos2, sin2, *tabs, g_ret)


def _ret_decode_kernel(q_ref, k_ref, v_ref, g_ref, cos_ref, sin_ref, gam_ref, gret_ref, s_ref, acc_ref, o_ref,
                       sn_ref):
    del acc_ref
    bt = q_ref.shape[0]
    assert 2 * bt == ITEMS
    cos2 = cos_ref[...]
    sin2 = sin_ref[...]
    lane = lax.broadcasted_iota(jnp.int32, (LANES, LANES), 1)
    zero = jnp.zeros((bt, HEAD_DIM), F32)
    one = jnp.ones((bt, HEAD_DIM), F32)
    for hp in range(RET_HEADS // 2):
        qr, kr, v, gam, sls = [], [], [], [], []
        for hh in range(2):
            h = 2 * hp + hh
            sl = slice(h * HEAD_DIM, (h + 1) * HEAD_DIM)
            sls.append(sl)
            gam.append(gam_ref[h][0:1, :])
            qr.append(_rope(q_ref[:, sl], cos2, sin2))
            kr.append(_rope(k_ref[:, sl], cos2, sin2) * (HEAD_DIM ** -0.5))
            v.append(v_ref[:, sl])
        k2 = [_bf16_parts(t, 2) for t in kr]
        q3 = [_bf16_parts(t, 3) for t in qr]
        v2 = [_bf16_parts(t, 2) for t in v]
        both = lambda parts, n: [parts[0][n], parts[1][n]]
        a = jnp.concatenate(both(k2, 0) + both(k2, 0) + both(k2, 1) + both(q3, 0) + both(q3, 1) + both(q3, 2)
                            + [zero] * 4, axis=0)
        cols = jnp.transpose(a)
        r_kv = jnp.concatenate(both(v2, 0) + both(v2, 1) + both(v2, 0) + [zero] * 10, axis=0)
        r_q = jnp.concatenate([zero] * 6 + [one] * 6 + [zero] * 4, axis=0)
        r = jnp.concatenate([r_kv, r_q], axis=1).astype(BF16)
        q_s = [[], []]
        for i in range(ITEMS):
            hh, j = divmod(i, bt)
            h = 2 * hp + hh
            out = _dot(jnp.where((lane & (ITEMS - 1)) == i, cols, 0.0).astype(BF16), r)
            s_old = s_ref[j, h]
            sn_ref[j, h] = gam[hh] * s_old + out[:, :HEAD_DIM]
            q_s[hh].append(jnp.sum(s_old * out[:, HEAD_DIM:], axis=0, keepdims=True))
        for hh in range(2):
            qk = jnp.sum(qr[hh] * kr[hh], axis=-1, keepdims=True)
            o = qk * v[hh] + jnp.concatenate(q_s[hh], axis=0) * gam[hh]
            o_ref[:, sls[hh]] = _head_norm_gate(o, gret_ref[:, sls[hh]], g_ref[:, sls[hh]]).astype(o_ref.dtype)


def _ret_decode(proj, states, acc, layer, cos2, sin2, gam_tab, g_ret, bt):
    nb = proj.shape[0]
    col = lambda off: pl.BlockSpec((bt, RET_DIM), lambda i, off=off: (i, off // RET_DIM))
    slab = pl.BlockSpec((None, bt, RET_HEADS, HEAD_DIM, HEAD_DIM), lambda i: (layer, i, 0, 0, 0))
    n_in = 9 if acc is None else 10
    return pl.pallas_call(
        functools.partial(_without_acc, _ret_decode_kernel, 9) if acc is None else _ret_decode_kernel,
        out_shape=(jax.ShapeDtypeStruct((nb, RET_DIM), BF16), jax.ShapeDtypeStruct(states.shape, F32)),
        grid=(nb // bt,),
        in_specs=[col(OFF_Q), col(OFF_K), col(OFF_V), col(OFF_GR),
                  pl.BlockSpec((1, HEAD_DIM), lambda i: (0, 0)), pl.BlockSpec((1, HEAD_DIM), lambda i: (0, 0)),
                  pl.BlockSpec((RET_HEADS, SUBLANES, LANES), lambda i: (0, 0, 0)),
                  pl.BlockSpec((1, RET_DIM), lambda i: (0, 0)), slab, pl.BlockSpec(memory_space=pl.ANY)][:n_in],
        out_specs=(pl.BlockSpec((bt, RET_DIM), lambda i: (i, 0)), slab),
        input_output_aliases={} if acc is None else {9: 1},
        compiler_params=_params("parallel"),
        name="ret_decode",
    )(*(proj, proj, proj, proj, cos2, sin2, gam_tab, g_ret, states, acc)[:n_in])


def _without_acc(kernel_fn, pos, *refs):
    return kernel_fn(*refs[:pos], None, *refs[pos:])


def _sc_prompt_kernel(h_ref, wbg_ref, wcg_ref, wsh_ref, wgs_ref, w_ref, b_ref, o_ref, st_ref, w16_ref, p_even,
                      p_odd):
    @pl.when(pl.program_id(1) == 0)
    def _():
        _load_weight_cols((wbg_ref, wcg_ref, wsh_ref, wgs_ref), w16_ref)

    width = wbg_ref.shape[0]

    def chunk(p, r0, n, tail):
        rows = pl.ds(pl.multiple_of(n * CHUNK, CHUNK), CHUNK)
        part = lambda i: p[r0:r0 + CHUNK, i * width:(i + 1) * width]
        u = part(1) * part(2)
        conv = _causal_conv_rows(tail, u, w_ref, b_ref)
        o_ref[rows, :] = (part(0) * conv * _silu(part(3))).astype(o_ref.dtype)
        return u[CHUNK - SUBLANES:, :]

    tail = _projected_chunks(h_ref, w16_ref, p_even, p_odd, chunk, jnp.zeros((SUBLANES, width), F32))
    st_ref[0] = tail[SUBLANES - (SC_WIDTH - 1):, :]


def _sc_prompt(h, w_t, layer, batch, seq, w, b):
    width = 512
    proj_rows = min(2 * PROJ_ROWS, seq)
    wrow = lambda off: pl.BlockSpec((None, width, D_MODEL), lambda c, bi, off=off: (layer, off // width + c, 0),
                                    pipeline_mode=pl.Buffered(1))
    return pl.pallas_call(
        _sc_prompt_kernel,
        out_shape=(jax.ShapeDtypeStruct((batch * seq, SC_DIM), BF16),
                   jax.ShapeDtypeStruct((batch, SC_WIDTH - 1, SC_DIM), F32)),
        grid=(SC_DIM // width, batch),
        in_specs=[pl.BlockSpec((seq, D_MODEL), lambda c, bi: (bi, 0)),
                  wrow(OFF_BG), wrow(OFF_CG), wrow(OFF_SH), wrow(OFF_GS),
                  pl.BlockSpec((SC_WIDTH, width), lambda c, bi: (0, c)),
                  pl.BlockSpec((1, width), lambda c, bi: (0, c))],
        out_specs=(pl.BlockSpec((seq, width), lambda c, bi: (bi, c)),
                   pl.BlockSpec((1, SC_WIDTH - 1, width), lambda c, bi: (bi, 0, c))),
        scratch_shapes=[pltpu.VMEM((D_MODEL, 4 * width), BF16),
                        pltpu.VMEM((proj_rows, 4 * width), F32), pltpu.VMEM((proj_rows, 4 * width), F32)],
        compiler_params=_params("parallel", "arbitrary"),
        name="sc_prompt",
    )(h, w_t, w_t, w_t, w_t, w, b)


def _sc_decode_kernel(bg_ref, cg_ref, sh_ref, gs_ref, st_ref, w_ref, b_ref, o_ref, stn_ref):
    u = cg_ref[...] * sh_ref[...]
    r0 = st_ref[:, 0:SC_DIM]
    r1 = st_ref[:, SC_DIM:]
    conv = b_ref[...] + r0 * w_ref[0:1, :] + r1 * w_ref[1:2, :] + u * w_ref[2:3, :]
    o_ref[...] = (bg_ref[...] * conv * _silu(gs_ref[...])).astype(o_ref.dtype)
    stn_ref[:, 0:SC_DIM] = r1
    stn_ref[:, SC_DIM:] = u


def _sc_decode(proj, state2d, w, b):
    nb = proj.shape[0]
    col = lambda off: pl.BlockSpec((nb, SC_DIM), lambda i, off=off: (0, off // SC_DIM))
    whole = lambda shape: pl.BlockSpec(shape, lambda i: (0,) * len(shape))
    return pl.pallas_call(
        _sc_decode_kernel,
        out_shape=(jax.ShapeDtypeStruct((nb, SC_DIM), BF16), jax.ShapeDtypeStruct(state2d.shape, F32)),
        grid=(1,),
        in_specs=[col(OFF_BG), col(OFF_CG), col(OFF_SH), col(OFF_GS), whole(state2d.shape),
                  whole((SC_WIDTH, SC_DIM)), whole((1, SC_DIM))],
        out_specs=(whole((nb, SC_DIM)), whole(state2d.shape)),
        compiler_params=_params("arbitrary"),
        name="sc_decode",
    )(proj, proj, proj, proj, state2d, w, b)


def _lane_cumsum(a, lane):
    s = 1
    while s < CHUNK:
        a = a + jnp.where(lane >= s, pltpu.roll(a, s, 1), 0.0)
        s *= 2
    return a


def _ssd_prompt_kernel(h_ref, wz_ref, wx_ref, wb_ref, wc_ref, wdt_ref, dtb_ref, alog_ref, dsk_ref,
                       cwx_ref, cwb_ref, cwc_ref, cbx_ref, cbb_ref, cbc_ref,
                       y_ref, stx_ref, stb_ref, stc_ref, sst_ref, w16_ref, p_even, p_odd, s_scr):
    @pl.when(pl.program_id(1) == 0)
    def _():
        _load_weight_cols((wz_ref, wx_ref, wb_ref, wc_ref, wdt_ref), w16_ref)

    gw = wz_ref.shape[0]
    n_st = wb_ref.shape[0]
    off_x, off_b, off_c, off_dt = gw, 2 * gw, 2 * gw + n_st, 2 * gw + 2 * n_st
    row = lax.broadcasted_iota(jnp.int32, (CHUNK, CHUNK), 0)
    lane = lax.broadcasted_iota(jnp.int32, (CHUNK, CHUNK), 1)
    lane8 = lax.broadcasted_iota(jnp.int32, (HEADS_PER_GROUP, CHUNK), 1)
    causal = row >= lane
    low_lanes = lane < SSM_HEAD_DIM
    low_rows = row < SSM_HEAD_DIM
    a_neg = -jnp.exp(alog_ref[0])
    dt_bias = dtb_ref[0]
    pad_heads = jnp.zeros((CHUNK - HEADS_PER_GROUP, CHUNK), F32)
    s_scr[...] = jnp.zeros_like(s_scr)

    def chunk(p, r0, n, tails):
        tx, tb, tc = tails
        rows = pl.ds(pl.multiple_of(n * CHUNK, CHUNK), CHUNK)
        loc = slice(r0, r0 + CHUNK)
        x_raw = p[loc, off_x:off_x + gw]
        b_raw = p[loc, off_b:off_b + n_st]
        c_raw = p[loc, off_c:off_c + n_st]
        xc = _silu(_causal_conv_rows(tx, x_raw, cwx_ref, cbx_ref))
        bb = _silu(_causal_conv_rows(tb, b_raw, cwb_ref, cbb_ref)).astype(BF16)
        cb16 = _silu(_causal_conv_rows(tc, c_raw, cwc_ref, cbc_ref)).astype(BF16)
        dt_t = _softplus(jnp.transpose(p[loc, off_dt:off_dt + LANES])[0:HEADS_PER_GROUP, :] + dt_bias)
        acum_t = _lane_cumsum(dt_t * a_neg, lane8)
        acum = jnp.transpose(jnp.concatenate([acum_t, pad_heads], axis=0))
        cb = _dot_nt(cb16, bb)
        y_intra, e_sel, dec_sel, xw, s_old = [], [], [], [], []
        for pp in range(PAIRS_PER_GROUP):
            x_pair = xc[:, pp * PAIR:(pp + 1) * PAIR]
            m, e_col, w_row, dec = [], [], [], []
            for hh in range(2):
                c = 2 * pp + hh
                a_col = jnp.broadcast_to(acum[:, c:c + 1], (CHUNK, CHUNK))
                a_row = acum_t[c:c + 1, :]
                dt_row = dt_t[c:c + 1, :]
                lmat = jnp.exp(jnp.where(causal, a_col - a_row, -jnp.inf))
                m.append((cb * lmat * dt_row).astype(BF16))
                e_col.append(jnp.exp(a_col))
                last = a_col[CHUNK - 1:CHUNK, :]
                w_row.append(jnp.exp(last - a_row) * dt_row)
                dec.append(jnp.exp(last))
            x_stack = jnp.concatenate([jnp.where(low_lanes, x_pair, 0.0), jnp.where(low_lanes, 0.0, x_pair)], axis=0)
            y_intra.append(_dot(jnp.concatenate(m, axis=1), x_stack.astype(BF16)))
            e_sel.append(jnp.where(low_lanes, e_col[0], e_col[1]))
            dec_sel.append(jnp.where(low_rows, dec[0], dec[1]))
            xw.append((jnp.transpose(x_pair) * jnp.where(low_rows, w_row[0], w_row[1])).astype(BF16))
            s_old.append(s_scr[pp])
        s_upd = _dot(jnp.concatenate(xw, axis=0), bb)
        for q in range(PAIRS_PER_GROUP // 2):
            s_two = jnp.concatenate([s_old[2 * q], s_old[2 * q + 1]], axis=0).astype(BF16)
            y_two = _dot_nt(cb16, s_two)
            for r in range(2):
                pp = 2 * q + r
                sl = slice(pp * PAIR, (pp + 1) * PAIR)
                y = y_intra[pp] + y_two[:, r * PAIR:(r + 1) * PAIR] * e_sel[pp] + dsk_ref[0][:, sl] * xc[:, sl]
                y_ref[rows, sl] = y * _silu(p[loc, sl])
                s_scr[pp] = dec_sel[pp] * s_old[pp] + s_upd[pp * PAIR:(pp + 1) * PAIR, :]
        cut = CHUNK - SUBLANES
        return x_raw[cut:, :], b_raw[cut:, :], c_raw[cut:, :]

    zeros = lambda width: jnp.zeros((SUBLANES, width), F32)
    tx, tb, tc = _projected_chunks(h_ref, w16_ref, p_even, p_odd, chunk, (zeros(gw), zeros(n_st), zeros(n_st)))
    keep = SUBLANES - (SSM_CONV - 1)
    stx_ref[0] = tx[keep:, :]
    stb_ref[0] = tb[keep:, :]
    stc_ref[0] = tc[keep:, :]
    for pp in range(PAIRS_PER_GROUP):
        for hh in range(2):
            sst_ref[0, 2 * pp + hh] = s_scr[pp, hh * SSM_HEAD_DIM:(hh + 1) * SSM_HEAD_DIM, :]


def _ssd_prompt(h, w_t, layer, batch, seq, dtb, alog, dsk, conv_w, conv_b):
    gw = SSM_DIM // SSM_GROUPS
    n = SSM_STATE
    wrow = lambda off, width: pl.BlockSpec((None, width, D_MODEL),
                                           lambda g, b, off=off, width=width: (layer, off // width + g, 0),
                                           pipeline_mode=pl.Buffered(1))
    grp = lambda rows, width: pl.BlockSpec((1, rows, width), lambda g, b: (g, 0, 0))
    cw = lambda off, width: pl.BlockSpec((SSM_CONV, width), lambda g, b, off=off, width=width: (0, off // width + g))
    cbias = lambda off, width: pl.BlockSpec((1, width), lambda g, b, off=off, width=width: (0, off // width + g))
    st = lambda width: pl.BlockSpec((1, SSM_CONV - 1, width), lambda g, b: (b, 0, g))
    n_proj = 2 * gw + 2 * n + LANES
    return pl.pallas_call(
        _ssd_prompt_kernel,
        out_shape=(jax.ShapeDtypeStruct((batch * seq, SSM_DIM), F32),
                   jax.ShapeDtypeStruct((batch, SSM_CONV - 1, SSM_DIM), F32),
                   jax.ShapeDtypeStruct((batch, SSM_CONV - 1, SSM_GROUPS * n), F32),
                   jax.ShapeDtypeStruct((batch, SSM_CONV - 1, SSM_GROUPS * n), F32),
                   jax.ShapeDtypeStruct((batch, SSM_HEADS, SSM_HEAD_DIM, n), F32)),
        grid=(SSM_GROUPS, batch),
        in_specs=[pl.BlockSpec((seq, D_MODEL), lambda g, b: (b, 0)),
                  wrow(OFF_Z, gw), wrow(OFF_X, gw), wrow(OFF_B, n), wrow(OFF_C, n), wrow(OFF_DT, HEADS_PER_GROUP),
                  grp(HEADS_PER_GROUP, CHUNK), grp(HEADS_PER_GROUP, CHUNK), grp(1, gw),
                  cw(0, gw), cw(SSM_DIM, n), cw(SSM_DIM + SSM_GROUPS * n, n),
                  cbias(0, gw), cbias(SSM_DIM, n), cbias(SSM_DIM + SSM_GROUPS * n, n)],
        out_specs=(pl.BlockSpec((seq, gw), lambda g, b: (b, g)), st(gw), st(n), st(n),
                   pl.BlockSpec((1, HEADS_PER_GROUP, SSM_HEAD_DIM, n), lambda g, b: (b, g, 0, 0))),
        scratch_shapes=[pltpu.VMEM((D_MODEL, n_proj), BF16),
                        pltpu.VMEM((PROJ_ROWS, n_proj), F32), pltpu.VMEM((PROJ_ROWS, n_proj), F32),
                        pltpu.VMEM((PAIRS_PER_GROUP, PAIR, n), F32)],
        compiler_params=_params("parallel", "arbitrary"),
        name="ssd_prompt",
    )(h, w_t, w_t, w_t, w_t, w_t, dtb, alog, dsk, conv_w, conv_w, conv_w, conv_b, conv_b, conv_b)


def _ssd_decode_kernel(z_ref, x_ref, bc_ref, dtr_ref, cst_ref, s_ref, cw_ref, cb_ref, dtb_ref, alog_ref, dsk_ref,
                       acc_ref, y_ref, cstn_ref, sn_ref):
    del acc_ref
    bt = z_ref.shape[0]
    xr = SSM_DIM // LANES
    gr = SSM_GROUPS * SSM_STATE // LANES
    u = jnp.concatenate([x_ref[...], bc_ref[...]], axis=1)
    conv = cb_ref[...]
    for j in range(SSM_CONV - 1):
        conv = conv + cst_ref[:, j] * cw_ref[j]
        cstn_ref[:, j] = u if j == SSM_CONV - 2 else cst_ref[:, j + 1]
    act = _silu(conv + u * cw_ref[SSM_CONV - 1])
    xs = act[:, 0:xr]
    bm = act[:, xr:xr + gr]
    cm = act[:, xr + gr:]
    dt = _softplus(dtr_ref[...] + dtb_ref[...])
    ea = jnp.exp(dt * -jnp.exp(alog_ref[...]))
    xdt = xs * dt
    assert xr == ITEMS == N_PAIRS
    lane = lax.broadcasted_iota(jnp.int32, (LANES, LANES), 1)
    row16 = lax.broadcasted_iota(jnp.int32, (xr, LANES), 0)
    zeros = lambda n: jnp.zeros((n, LANES), F32)
    r_ea = jnp.concatenate([zeros(3 * ITEMS), jnp.ones((3 * ITEMS, LANES), F32), zeros(2 * ITEMS)], axis=0)
    for j in range(bt):
        cbt = zeros(xr)
        b_rows = zeros(xr)
        for g in range(SSM_GROUPS):
            in_group = row16 // PAIRS_PER_GROUP == g
            cb_g = jnp.sum(cm[j, g:g + 1, :] * bm[j, g:g + 1, :], axis=-1, keepdims=True)
            cbt = jnp.where(in_group, cb_g, cbt)
            b_rows = jnp.where(in_group, bm[j, g:g + 1, :], b_rows)
        xd2 = _bf16_parts(xdt[j], 2)
        ea3 = _bf16_parts(ea[j], 3)
        b2 = _bf16_parts(b_rows, 2)
        a = jnp.concatenate([xd2[0], xd2[0], xd2[1]] + ea3 + [zeros(2 * ITEMS)], axis=0)
        cols = jnp.transpose(a)
        r_xb = jnp.concatenate([b2[0], b2[1], b2[0], zeros(5 * ITEMS)], axis=0)
        r = jnp.concatenate([r_xb, r_ea], axis=1).astype(BF16)
        y_t = jnp.zeros((LANES, LANES), F32)
        for pp in range(N_PAIRS):
            g = pp // PAIRS_PER_GROUP
            out = _dot(jnp.where((lane & (ITEMS - 1)) == pp, cols, 0.0).astype(BF16), r)
            s_old = s_ref[j, 2 * pp:2 * pp + 2].reshape(PAIR, SSM_STATE)
            y_col = jnp.sum(s_old * cm[j, g:g + 1, :], axis=-1, keepdims=True)
            y_t = jnp.where(lane == pp, y_col, y_t)
            s_new = out[:, SSM_STATE:] * s_old + out[:, :SSM_STATE]
            sn_ref[j, 2 * pp:2 * pp + 2] = s_new.reshape(2, SSM_HEAD_DIM, SSM_STATE)
        y_inter = jnp.transpose(y_t)[0:xr, :]
        y = cbt * xdt[j] + y_inter * ea[j] + dsk_ref[0] * xs[j]
        y_ref[j] = y * _silu(z_ref[j])


def _ssd_decode(proj3, dt_raw3, cstates, states, acc, layer, conv_w3, conv_b3, dtb3, alog3, dsk3, bt):
    nb = proj3.shape[0]
    xr = SSM_DIM // LANES
    cr = SSM_CONV_DIM // LANES
    whole = lambda shape: pl.BlockSpec(shape, lambda i: (0,) * len(shape))
    cst = pl.BlockSpec((bt, SSM_CONV - 1, cr, LANES), lambda i: (i, 0, 0, 0))
    st_in = pl.BlockSpec((None, bt, SSM_HEADS, SSM_HEAD_DIM, SSM_STATE), lambda i: (layer, i, 0, 0, 0))
    cst_in = pl.BlockSpec((None, bt, SSM_CONV - 1, cr, LANES), lambda i: (layer, i, 0, 0, 0))
    n_in = 11 if acc is None else 12
    return pl.pallas_call(
        functools.partial(_without_acc, _ssd_decode_kernel, 11) if acc is None else _ssd_decode_kernel,
        out_shape=(jax.ShapeDtypeStruct((nb, xr, LANES), F32), jax.ShapeDtypeStruct(cstates.shape[1:], F32),
                   jax.ShapeDtypeStruct(states.shape, F32)),
        grid=(nb // bt,),
        in_specs=[pl.BlockSpec((bt, xr, LANES), lambda i: (i, OFF_Z // SSM_DIM, 0)),
                  pl.BlockSpec((bt, xr, LANES), lambda i: (i, OFF_X // SSM_DIM, 0)),
                  pl.BlockSpec((bt, SUBLANES, LANES), lambda i: (i, OFF_B // (SUBLANES * LANES), 0)),
                  pl.BlockSpec((bt, xr, LANES), lambda i: (i, 0, 0)),
                  cst_in, st_in, whole((SSM_CONV, cr, LANES)), whole((1, cr, LANES)),
                  whole((1, xr, LANES)), whole((1, xr, LANES)), whole((1, xr, LANES)),
                  pl.BlockSpec(memory_space=pl.ANY)][:n_in],
        out_specs=(pl.BlockSpec((bt, xr, LANES), lambda i: (i, 0, 0)), cst, st_in),
        input_output_aliases={} if acc is None else {11: 2},
        compiler_params=_params("parallel"),
        name="ssd_decode",
    )(*(proj3, proj3, proj3, dt_raw3, cstates, states, conv_w3, conv_b3, dtb3, alog3, dsk3, acc)[:n_in])


def _rope_tables(pos):
    half = HEAD_DIM // 2
    inv = ROPE_BASE ** (-jnp.arange(half, dtype=F32) / half)
    ang = pos.astype(F32)[:, None] * inv[None, :]
    cos, sin = jnp.cos(ang), jnp.sin(ang)
    return jnp.concatenate([cos, cos], axis=-1), jnp.concatenate([-sin, sin], axis=-1)


def _retention_tables():
    lg = jnp.log(1.0 - 2.0 ** (-5.0 - jnp.arange(RET_HEADS, dtype=F32)))
    i = jnp.arange(CHUNK, dtype=F32)
    diff = i[:, None] - i[None, :]
    dmat = jnp.exp(jnp.where((diff >= 0)[None], diff[None] * lg[:, None, None], -jnp.inf))
    full = lambda t: jnp.broadcast_to(t, (RET_HEADS, CHUNK, CHUNK))
    q_dec = full(jnp.exp((i + 1.0)[None, :, None] * lg[:, None, None]))
    k_dec = full(jnp.exp((CHUNK - 1.0 - i)[None, :, None] * lg[:, None, None]))
    c_dec = full(jnp.exp(CHUNK * lg)[:, None, None])
    gam = jnp.broadcast_to(jnp.exp(lg)[:, None, None], (RET_HEADS, SUBLANES, LANES))
    return (dmat, q_dec, k_dec, c_dec), gam


def _per_group(v):
    v = v.astype(F32).reshape(SSM_GROUPS, HEADS_PER_GROUP, 1)
    return jnp.broadcast_to(v, (SSM_GROUPS, HEADS_PER_GROUP, CHUNK))


def _per_lane(v):
    return jnp.repeat(v.astype(F32), SSM_HEAD_DIM)[None, :]


def kernel(x_prompt, x_sample, state_ret, state_sconv, state_ssm_conv, state_ssm, w_in, w_out, norm_pre, norm_post,
           ret_norm, sc_conv_w, sc_conv_b, ssm_conv_w, ssm_conv_b, ssm_dt_bias, ssm_a_log, ssm_d, ssm_norm):
    batch, seq, _ = x_prompt.shape
    nb = x_sample.shape[0]
    depth = w_in.shape[0]
    mp = batch * seq
    xr = SSM_DIM // LANES
    cr = SSM_CONV_DIM // LANES

    cos_p, sin_p = _rope_tables(jnp.arange(seq))
    cos_s, sin_s = _rope_tables(PAST_LEN + jnp.arange(1))
    ret_tabs, gam_tab = _retention_tables()

    w_out16 = w_out.astype(BF16)
    w_t = jnp.swapaxes(w_in, 1, 2)
    w_dt_lane = jnp.repeat(w_t[:, OFF_DT:, :], SSM_HEAD_DIM, axis=1).astype(BF16)

    tm_o = min(2 * OUT_SUB_ROWS, mp)
    xp = x_prompt.reshape(mp, D_MODEL)
    xs = x_sample.reshape(nb, D_MODEL)
    hp = _prenorm(xp, norm_pre[0][None, :], tm_o)
    hs = _prenorm(xs, norm_pre[0][None, :], nb)

    outs = [[] for _ in range(6)]
    ret_acc = None
    ssm_acc = None
    for l in range(depth):
        g_next = norm_pre[(l + 1) % depth][None, :]
        dtb_g, alog_g = _per_group(ssm_dt_bias[l]), _per_group(ssm_a_log[l])
        dsk_lane = _per_lane(ssm_d[l])
        dsk_g = dsk_lane.reshape(SSM_GROUPS, 1, SSM_DIM // SSM_GROUPS)
        conv_b = ssm_conv_b[l][None, :]

        o_ret, r_new = _ret_prompt(hp, w_t, l, batch, seq, cos_p, sin_p, ret_tabs, ret_norm[l][None, :])
        o_sc, c_new = _sc_prompt(hp, w_t, l, batch, seq, sc_conv_w[l], sc_conv_b[l][None, :])
        ssm_pre, cx, cb_, cc, s_new = _ssd_prompt(hp, w_t, l, batch, seq, dtb_g, alog_g, dsk_g, ssm_conv_w[l], conv_b)
        xp, hp = _outproj(o_ret, o_sc, ssm_pre, xp, w_out16, l, ssm_norm[l][None, :], norm_post[l][None, :],
                          g_next, tm_o)
        outs[0].append(r_new)
        outs[1].append(c_new)
        outs[2].append(jnp.concatenate([cx, cb_, cc], axis=-1))
        outs[3].append(s_new)

        proj_s = _matmul_nt(hs, w_t, l, N_MAIN, 1024, "inproj_decode")
        dt_s = _matmul_nt(hs, w_dt_lane, l, SSM_DIM, SSM_DIM, "dtproj_decode")
        o_ret, ret_acc = _ret_decode(proj_s, state_ret, ret_acc, l, cos_s, sin_s, gam_tab, ret_norm[l][None, :],
                                     SUBLANES)
        o_sc, c_new = _sc_decode(proj_s, state_sconv[l].reshape(nb, (SC_WIDTH - 1) * SC_DIM), sc_conv_w[l],
                                 sc_conv_b[l][None, :])
        y_s, cs_new, ssm_acc = _ssd_decode(
            proj_s.reshape(nb, N_MAIN // LANES, LANES), dt_s.reshape(nb, xr, LANES),
            state_ssm_conv.reshape(depth, nb, SSM_CONV - 1, cr, LANES), state_ssm, ssm_acc, l,
            ssm_conv_w[l].reshape(SSM_CONV, cr, LANES), conv_b.reshape(1, cr, LANES),
            _per_lane(ssm_dt_bias[l]).reshape(1, xr, LANES), _per_lane(ssm_a_log[l]).reshape(1, xr, LANES),
            dsk_lane.reshape(1, xr, LANES), 4)
        xs, hs = _outproj(o_ret, o_sc, y_s.reshape(nb, SSM_DIM), xs, w_out16, l, ssm_norm[l][None, :],
                          norm_post[l][None, :], g_next, nb)
        outs[4].append(c_new.reshape(nb, SC_WIDTH - 1, SC_DIM))
        outs[5].append(cs_new.reshape(nb, SSM_CONV - 1, SSM_CONV_DIM))

    stacked = [jnp.stack(o) for o in outs]
    return (xp.reshape(batch, seq, D_MODEL), xs.reshape(nb, 1, D_MODEL), *stacked[:4],
            ret_acc, stacked[4], stacked[5], ssm_acc)
```
